```python
import math
import jax, jax.numpy as jnp
from jax import lax
import numpy as np

D_MODEL = 4096
BATCH = 16
SEQ = 256
DEPTH = 2
DEC_BATCH = 8
DEC_SEQ = 2048
PAST_LEN = 256

GRID_W = 64
CHUNK = 32
CONV_K = 3
ROPE_BASE = 10000.0
N_AB = (DEPTH + 1) // 2
N_CD = DEPTH // 2
MIX_W = D_MODEL
H_A = 8
V_A = (MIX_W // 2) // H_A
K_A = V_A // 2
GLA_RANK = 16
GLA_GATE_NORM = 16.0
H_B = 16
K_B = (MIX_W // 2) // H_B
V_B = K_B
H_C = 8
V_C = (MIX_W // 2) // H_C
K_C = V_C // 2
P_D = 64
H_D = (MIX_W // 2) // P_D
N_D = 128
G_D = 4
WA_QK = H_A * K_A
WA_V = H_A * V_A
WB = H_B * K_B
WC_QK = H_C * K_C
WC_V = H_C * V_C
WD = H_D * P_D
WD_BC = G_D * N_D
AB_SIZES = (WA_QK, WA_QK, WA_V, WA_V, GLA_RANK, GLA_RANK, 3 * WB, WB, H_B, H_B, H_B, H_B)
CD_SIZES = (WC_QK, WC_QK, WC_V, WC_V, WD + 2 * WD_BC, WD, H_D, H_D)
IN_AB = sum(AB_SIZES)
IN_CD = sum(CD_SIZES)
OUT_AB = WA_V + H_B * V_B
OUT_CD = WC_V + WD

kernel_name = "bidir_gla_gdn_retnet_ssd_diffusion_step"


def split_points(sizes):
    return [int(v) for v in np.cumsum(sizes)[:-1]]


def rev(a):
    return jnp.flip(a, axis=1)


def rms_norm(x, g, eps=1e-6):
    xf = x.astype(jnp.float32)
    y = xf * lax.rsqrt(jnp.mean(xf * xf, axis=-1, keepdims=True) + eps)
    return y.astype(x.dtype) * g


def layer_norm(x, g, b, eps=1e-5):
    xf = x.astype(jnp.float32)
    mu = jnp.mean(xf, axis=-1, keepdims=True)
    var = jnp.mean(jnp.square(xf - mu), axis=-1, keepdims=True)
    return ((xf - mu) * lax.rsqrt(var + eps)).astype(x.dtype) * g + b


def l2_normalize(x, eps=1e-6):
    return x * lax.rsqrt(jnp.sum(x * x, axis=-1, keepdims=True) + eps)


def dwconv_centred(x, w, b=None):
    pad = (CONV_K - 1) // 2
    y = lax.conv_general_dilated(x, w[:, None, :].astype(x.dtype), (1,), [(pad, pad)],
                                 dimension_numbers=("NWC", "WIO", "NWC"),
                                 feature_group_count=x.shape[-1])
    return y if b is None else y + b


def axial_rotary(x):
    t, k = x.shape[1], x.shape[-1]
    n_rows = t // GRID_W
    rows = jnp.repeat(jnp.arange(n_rows), GRID_W).astype(jnp.float32)
    cols = jnp.tile(jnp.arange(GRID_W), n_rows).astype(jnp.float32)
    n_freq = k // 4
    inv_freq = ROPE_BASE ** (-jnp.arange(n_freq, dtype=jnp.float32) / n_freq)
    ang = jnp.concatenate([rows[:, None] * inv_freq, cols[:, None] * inv_freq], axis=-1)
    cos, sin = jnp.cos(ang)[None, :, None], jnp.sin(ang)[None, :, None]
    xr = x.reshape(*x.shape[:-1], k // 2, 2)
    x1, x2 = xr[..., 0], xr[..., 1]
    return jnp.stack([x1 * cos - x2 * sin, x1 * sin + x2 * cos], axis=-1).reshape(x.shape)


def to_chunks(x):
    b, t = x.shape[:2]
    return jnp.swapaxes(x.reshape(b, t // CHUNK, CHUNK, *x.shape[2:]), 0, 1)


def from_chunks(y):
    n, b = y.shape[:2]
    return jnp.swapaxes(y, 0, 1).reshape(b, n * CHUNK, *y.shape[3:])


def scalar_decay_scan(q, k, v, log_a, s0):
    mask = jnp.tril(jnp.ones((CHUNK, CHUNK), dtype=bool))[None, :, :, None]

    def step(s, xs):
        qc, kc, vc, lc = xs
        g = jnp.cumsum(lc, axis=1)
        diff = g[:, :, None, :] - g[:, None, :, :]
        decay = jnp.where(mask, jnp.exp(jnp.where(mask, diff, 0.0)), 0.0)
        scores = jnp.einsum("bihk,bjhk->bijh", qc, kc) * decay
        o = (jnp.einsum("bijh,bjhv->bihv", scores, vc)
             + jnp.einsum("bihk,bhkv->bihv", qc * jnp.exp(g)[..., None], s))
        g_last = g[:, -1]
        s = (s * jnp.exp(g_last)[..., None, None]
             + jnp.einsum("bjhk,bjhv->bhkv", kc * jnp.exp(g_last[:, None] - g)[..., None], vc))
        return s, o

    s_fin, o = lax.scan(step, s0.astype(jnp.float32),
                        (to_chunks(q), to_chunks(k), to_chunks(v), to_chunks(log_a)))
    return from_chunks(o), s_fin


def vector_decay_scan(q, k, v, log_a, s0):
    mask = jnp.tril(jnp.ones((CHUNK, CHUNK), dtype=bool))[None, :, :, None, None]

    def step(s, xs):
        qc, kc, vc, lc = xs
        g = jnp.cumsum(lc, axis=1)
        decay = jnp.exp(jnp.where(mask, g[:, :, None] - g[:, None], -jnp.inf))
        scores = jnp.einsum("bijhk,bjhk->bijh", qc[:, :, None] * decay, kc)
        o = (jnp.einsum("bijh,bjhv->bihv", scores, vc)
             + jnp.einsum("bihk,bhkv->bihv", qc * jnp.exp(g), s))
        g_last = g[:, -1]
        s = (s * jnp.exp(g_last)[..., None]
             + jnp.einsum("bjhk,bjhv->bhkv", kc * jnp.exp(g_last[:, None] - g), vc))
        return s, o

    s_fin, o = lax.scan(step, s0.astype(jnp.float32),
                        (to_chunks(q), to_chunks(k), to_chunks(v), to_chunks(log_a)))
    return from_chunks(o), s_fin


def delta_scan(q, k, v, beta, log_a, s0):
    tri = jnp.tril(jnp.ones((CHUNK, CHUNK), dtype=bool))
    stri = jnp.tril(jnp.ones((CHUNK, CHUNK), dtype=bool), -1)
    eye = jnp.eye(CHUNK, dtype=jnp.float32)

    def step(s, xs):
        qc, kc, vc, bc, lc = xs
        g = jnp.cumsum(lc, axis=1)
        gh = jnp.swapaxes(g, 1, 2)
        diff = gh[..., :, None] - gh[..., None, :]
        decay = jnp.where(tri, jnp.exp(jnp.where(tri, diff, 0.0)), 0.0)
        kbeta = kc * bc[..., None]
        lower = jnp.where(stri, jnp.einsum("bihk,bjhk->bhij", kbeta, kc) * decay, 0.0)
        t_inv = lax.linalg.triangular_solve(eye + lower, jnp.broadcast_to(eye, lower.shape),
                                            left_side=True, lower=True, unit_diagonal=True)
        u = jnp.einsum("bhij,bjhv->bihv", t_inv, vc * bc[..., None])
        w = jnp.einsum("bhij,bjhk->bihk", t_inv, kbeta * jnp.exp(g)[..., None])
        v_new = u - jnp.einsum("bihk,bhkv->bihv", w, s)
        attn = jnp.einsum("bihk,bjhk->bhij", qc, kc) * decay
        o = (jnp.einsum("bihk,bhkv->bihv", qc * jnp.exp(g)[..., None], s)
             + jnp.einsum("bhij,bjhv->bihv", attn, v_new))
        g_last = g[:, -1]
        s = (s * jnp.exp(g_last)[..., None, None]
             + jnp.einsum("bjhk,bjhv->bhkv", kc * jnp.exp(g_last[:, None] - g)[..., None], v_new))
        return s, o

    s_fin, o = lax.scan(step, s0.astype(jnp.float32),
                        (to_chunks(q), to_chunks(k), to_chunks(v), to_chunks(beta), to_chunks(log_a)))
    return from_chunks(o), s_fin


def mixer_ab(h, w_in, w_out, gla_w2, gla_b, gla_ng, gdn_conv, gdn_a_log, gdn_dt_bias, gdn_ng,
             st_gla, st_gdn):
    f32 = jnp.float32
    bsz, t, _ = h.shape
    (qa, ka, va, za, ra_f, ra_b, qkv_b, zb, beta_fw, beta_bw, a_fw, a_bw) = jnp.split(
        h @ w_in, split_points(AB_SIZES), axis=-1)
    qa = qa.astype(f32).reshape(bsz, t, H_A, K_A) * K_A ** -0.5
    ka = ka.astype(f32).reshape(bsz, t, H_A, K_A)
    va = va.astype(f32).reshape(bsz, t, H_A, V_A)

    def gla_log_decay(r, d):
        logit = (r @ gla_w2[d] + gla_b[d]).astype(f32)
        return (jax.nn.log_sigmoid(logit) / GLA_GATE_NORM).reshape(bsz, t, H_A, K_A)

    oa_f, sa_f = vector_decay_scan(qa, ka, va, gla_log_decay(ra_f, 0), st_gla[0])
    oa_b, sa_b = vector_decay_scan(rev(qa), rev(ka), rev(va), rev(gla_log_decay(ra_b, 1)), st_gla[1])
    ya = (rms_norm(oa_f + rev(oa_b), gla_ng.astype(f32)).reshape(bsz, t, WA_V)
          * jax.nn.silu(za.astype(f32)))
    qkv_b = jax.nn.silu(dwconv_centred(qkv_b, gdn_conv)).astype(f32)
    qb, kb, vb = jnp.split(qkv_b, 3, axis=-1)
    qb = l2_normalize(qb.reshape(bsz, t, H_B, K_B)) * K_B ** -0.5
    kb = l2_normalize(kb.reshape(bsz, t, H_B, K_B))
    vb = vb.reshape(bsz, t, H_B, V_B)

    def gdn_gates(b_raw, a_raw, d):
        beta = jax.nn.sigmoid(b_raw.astype(f32))
        log_a = -jnp.exp(gdn_a_log[d].astype(f32)) * jax.nn.softplus(
            a_raw.astype(f32) + gdn_dt_bias[d].astype(f32))
        return beta, log_a

    bt_f, la_f = gdn_gates(beta_fw, a_fw, 0)
    bt_b, la_b = gdn_gates(beta_bw, a_bw, 1)
    ob_f, sb_f = delta_scan(qb, kb, vb, bt_f, la_f, st_gdn[0])
    ob_b, sb_b = delta_scan(rev(qb), rev(kb), rev(vb), rev(bt_b), rev(la_b), st_gdn[1])
    yb = (rms_norm(ob_f + rev(ob_b), gdn_ng.astype(f32)).reshape(bsz, t, H_B * V_B)
          * jax.nn.silu(zb.astype(f32)))
    out = jnp.concatenate([ya, yb], axis=-1).astype(h.dtype) @ w_out
    return out, ((sa_f, sa_b), (sb_f, sb_b))


def mixer_cd(h, w_in, w_out, ret_ng, ret_nb, ssd_conv_w, ssd_conv_b, ssd_a_log, ssd_dt_bias, ssd_d,
             ssd_ng, st_ret, st_ssd, on_grid):
    f32 = jnp.float32
    bsz, t, _ = h.shape
    qc, kc, vc, zc, xbc, zd, dt_fw, dt_bw = jnp.split(h @ w_in, split_points(CD_SIZES), axis=-1)
    qc = qc.astype(f32).reshape(bsz, t, H_C, K_C)
    kc = kc.astype(f32).reshape(bsz, t, H_C, K_C) * K_C ** -0.5
    vc = vc.astype(f32).reshape(bsz, t, H_C, V_C)
    if on_grid:
        qc, kc = axial_rotary(qc), axial_rotary(kc)
    log_gamma = jnp.log1p(-jnp.exp2(-5.0 - jnp.arange(H_C, dtype=f32)))
    lg_f = jnp.broadcast_to(log_gamma, (bsz, t, H_C))
    lg_b = jnp.broadcast_to(log_gamma[::-1], (bsz, t, H_C))
    oc_f, sc_f = scalar_decay_scan(qc, kc, vc, lg_f, st_ret[0])
    oc_b, sc_b = scalar_decay_scan(rev(qc), rev(kc), rev(vc), lg_b, st_ret[1])
    yc = (layer_norm(oc_f + rev(oc_b), ret_ng.astype(f32), ret_nb.astype(f32)).reshape(bsz, t, WC_V)
          * jax.nn.silu(zc.astype(f32)))
    xbc = jax.nn.silu(dwconv_centred(xbc, ssd_conv_w, ssd_conv_b)).astype(f32)
    xd, bd, cm = jnp.split(xbc, [WD, WD + WD_BC], axis=-1)
    xd = xd.reshape(bsz, t, H_D, P_D)
    bd = jnp.repeat(bd.reshape(bsz, t, G_D, N_D), H_D // G_D, axis=2)
    cm = jnp.repeat(cm.reshape(bsz, t, G_D, N_D), H_D // G_D, axis=2)

    def ssd_dt(raw, d):
        dt = jax.nn.softplus(raw.astype(f32) + ssd_dt_bias[d].astype(f32))
        return dt, -jnp.exp(ssd_a_log[d].astype(f32)) * dt

    dt_f, la_f = ssd_dt(dt_fw, 0)
    dt_b, la_b = ssd_dt(dt_bw, 1)
    od_f, sd_f = scalar_decay_scan(cm, bd, xd * dt_f[..., None], la_f, st_ssd[0])
    od_b, sd_b = scalar_decay_scan(rev(cm), rev(bd), rev(xd * dt_b[..., None]), rev(la_b), st_ssd[1])
    yd = ((od_f + rev(od_b) + ssd_d.astype(f32)[:, None] * xd).reshape(bsz, t, WD)
          * jax.nn.silu(zd.astype(f32)))
    yd = rms_norm(yd.reshape(bsz, t, G_D, WD // G_D),
                  ssd_ng.astype(f32).reshape(G_D, WD // G_D)).reshape(bsz, t, WD)
    out = jnp.concatenate([yc, yd], axis=-1).astype(h.dtype) @ w_out
    return out, ((sc_f, sc_b), (sd_f, sd_b))


def zero_states(l, bsz):
    if l % 2 == 0:
        shapes = ((H_A, K_A, V_A), (H_B, K_B, V_B))
    else:
        shapes = ((H_C, K_C, V_C), (H_D, N_D, P_D))
    return tuple((jnp.zeros((bsz,) + s, jnp.float32), jnp.zeros((bsz,) + s, jnp.float32)) for s in shapes)


def run_trunk(x, cvec, init_states, on_grid, ada_w, ada_b, norm_g, final_norm_g, ab, cd):
    final_states = []
    for l in range(DEPTH):
        mod = (jax.nn.silu(cvec) @ ada_w[l] + ada_b[l])[:, None, :]
        shift, scale, gate = jnp.split(mod, 3, axis=-1)
        h = rms_norm(x, norm_g[l]) * (1.0 + scale) + shift
        if l % 2 == 0:
            out, st = mixer_ab(h, *[p[l // 2] for p in ab], *init_states[l])
        else:
            out, st = mixer_cd(h, *[p[l // 2] for p in cd], *init_states[l], on_grid)
        x = x + gate * out
        final_states.append(st)
    return rms_norm(x, final_norm_g), final_states


def setup_inputs(seed: int = 0) -> dict:
    key = jax.random.key(seed)
    ks = iter(jax.random.split(key, 48))
    f32 = jnp.float32

    def nrm(shape, s):
        return jax.random.normal(next(ks), shape, f32) * s

    def gain(shape):
        return 1.0 + nrm(shape, 0.02)

    def a_log(shape):
        return jnp.log(jax.random.uniform(next(ks), shape, f32, 1.0, 16.0))

    def dt_bias(shape):
        dt = jnp.exp(jax.random.uniform(next(ks), shape, f32, math.log(1e-3), math.log(1e-1)))
        return dt + jnp.log(-jnp.expm1(-dt))

    return {
        "x_prompt": nrm((BATCH, SEQ, D_MODEL), 1.0),
        "x_sample": nrm((DEC_BATCH, DEC_SEQ, D_MODEL), 1.0),
        "state_gla_fwd": nrm((DEC_BATCH, N_AB, H_A, K_A, V_A), K_A ** -0.5),
        "state_gla_bwd": nrm((DEC_BATCH, N_AB, H_A, K_A, V_A), K_A ** -0.5),
        "state_gdn_fwd": nrm((DEC_BATCH, N_AB, H_B, K_B, V_B), K_B ** -0.5),
        "state_gdn_bwd": nrm((DEC_BATCH, N_AB, H_B, K_B, V_B), K_B ** -0.5),
        "state_ret_fwd": nrm((DEC_BATCH, N_CD, H_C, K_C, V_C), K_C ** -0.5),
        "state_ret_bwd": nrm((DEC_BATCH, N_CD, H_C, K_C, V_C), K_C ** -0.5),
        "state_ssd_fwd": nrm((DEC_BATCH, N_CD, H_D, N_D, P_D), N_D ** -0.5),
        "state_ssd_bwd": nrm((DEC_BATCH, N_CD, H_D, N_D, P_D), N_D ** -0.5),
        "c": nrm((DEC_BATCH, D_MODEL), 1.0),
        "c_ctx": nrm((D_MODEL,), 1.0),
        "ada_w": nrm((DEPTH, D_MODEL, 3 * D_MODEL), 0.5 * D_MODEL ** -0.5),
        "ada_b": nrm((DEPTH, 3 * D_MODEL), 0.02),
        "norm_g": gain((DEPTH, D_MODEL)),
        "final_norm_g": gain((D_MODEL,)),
        "ab_w_in": nrm((N_AB, D_MODEL, IN_AB), D_MODEL ** -0.5),
        "ab_w_out": nrm((N_AB, OUT_AB, D_MODEL), OUT_AB ** -0.5),
        "gla_gate_w2": nrm((N_AB, 2, GLA_RANK, WA_QK), GLA_RANK ** -0.5),
        "gla_gate_b": 1.0 + nrm((N_AB, 2, WA_QK), 0.5),
        "gla_norm_g": gain((N_AB, V_A)),
        "gdn_conv_w": nrm((N_AB, CONV_K, 3 * WB), CONV_K ** -0.5),
        "gdn_a_log": a_log((N_AB, 2, H_B)),
        "gdn_dt_bias": dt_bias((N_AB, 2, H_B)),
        "gdn_norm_g": gain((N_AB, V_B)),
        "cd_w_in": nrm((N_CD, D_MODEL, IN_CD), D_MODEL ** -0.5),
        "cd_w_out": nrm((N_CD, OUT_CD, D_MODEL), OUT_CD ** -0.5),
        "ret_norm_g": gain((N_CD, V_C)),
        "ret_norm_b": nrm((N_CD, V_C), 0.02),
        "ssd_conv_w": nrm((N_CD, CONV_K, WD + 2 * WD_BC), CONV_K ** -0.5),
        "ssd_conv_b": nrm((N_CD, WD + 2 * WD_BC), 0.02),
        "ssd_a_log": a_log((N_CD, 2, H_D)),
        "ssd_dt_bias": dt_bias((N_CD, 2, H_D)),
        "ssd_d": gain((N_CD, H_D)),
        "ssd_norm_g": gain((N_CD, WD)),
    }


def reference(x_prompt, x_sample, state_gla_fwd, state_gla_bwd, state_gdn_fwd, state_gdn_bwd,
              state_ret_fwd, state_ret_bwd, state_ssd_fwd, state_ssd_bwd, c, c_ctx,
              ada_w, ada_b, norm_g, final_norm_g,
              ab_w_in, ab_w_out, gla_gate_w2, gla_gate_b, gla_norm_g,
              gdn_conv_w, gdn_a_log, gdn_dt_bias, gdn_norm_g,
              cd_w_in, cd_w_out, ret_norm_g, ret_norm_b,
              ssd_conv_w, ssd_conv_b, ssd_a_log, ssd_dt_bias, ssd_d, ssd_norm_g):
    ab = (ab_w_in, ab_w_out, gla_gate_w2, gla_gate_b, gla_norm_g,
          gdn_conv_w, gdn_a_log, gdn_dt_bias, gdn_norm_g)
    cd = (cd_w_in, cd_w_out, ret_norm_g, ret_norm_b,
          ssd_conv_w, ssd_conv_b, ssd_a_log, ssd_dt_bias, ssd_d, ssd_norm_g)

    ctx_init = [zero_states(l, x_prompt.shape[0]) for l in range(DEPTH)]
    y_prompt, ctx_states = run_trunk(x_prompt, c_ctx[None, :], ctx_init, False,
                                     ada_w, ada_b, norm_g, final_norm_g, ab, cd)

    lat_init = []
    for l in range(DEPTH):
        i = l // 2
        if l % 2 == 0:
            lat_init.append(((state_gla_fwd[:, i], state_gla_bwd[:, i]),
                             (state_gdn_fwd[:, i], state_gdn_bwd[:, i])))
        else:
            lat_init.append(((state_ret_fwd[:, i], state_ret_bwd[:, i]),
                             (state_ssd_fwd[:, i], state_ssd_bwd[:, i])))
    y_sample, _ = run_trunk(x_sample, c, lat_init, True,
                            ada_w, ada_b, norm_g, final_norm_g, ab, cd)

    ab_st = ctx_states[0::2]
    cd_st = ctx_states[1::2]

    def stack(states, m, d):
        return jnp.stack([s[m][d] for s in states], axis=1).astype(x_prompt.dtype)

    new_gla_fwd = stack(ab_st, 0, 0)
    new_gla_bwd = stack(ab_st, 0, 1)
    new_gdn_fwd = stack(ab_st, 1, 0)
    new_gdn_bwd = stack(ab_st, 1, 1)
    new_ret_fwd = stack(cd_st, 0, 0)
    new_ret_bwd = stack(cd_st, 0, 1)
    new_ssd_fwd = stack(cd_st, 1, 0)
    new_ssd_bwd = stack(cd_st, 1, 1)
    return (y_prompt, y_sample, new_gla_fwd, new_gla_bwd, new_gdn_fwd, new_gdn_bwd,
            new_ret_fwd, new_ret_bwd, new_ssd_fwd, new_ssd_bwd)
```

```python
import functools
import math

import jax
import jax.numpy as jnp
from jax import lax
from jax.experimental import pallas as pl
from jax.experimental.pallas import tpu as pltpu

F32 = jnp.float32
MXU_DTYPE = jnp.bfloat16

D_MODEL = 4096
DEPTH = 2
GRID_W = 64
ROPE_BASE = 10000.0
H_A, K_A, V_A = 8, 128, 256
GLA_RANK = 16
GLA_GATE_NORM = 16.0
H_B, K_B, V_B = 16, 128, 128
H_C, K_C, V_C = 8, 128, 256
H_D, N_D, P_D, G_D = 32, 128, 64, 4
HEADS_PER_GROUP = H_D // G_D
WA_QK, WA_V = H_A * K_A, H_A * V_A
WB = H_B * K_B
WC_QK, WC_V = H_C * K_C, H_C * V_C
WD, WD_BC = H_D * P_D, G_D * N_D
AB_SIZES = (WA_QK, WA_QK, WA_V, WA_V, GLA_RANK, GLA_RANK, 3 * WB, WB, H_B, H_B, H_B, H_B)
CD_SIZES = (WC_QK, WC_QK, WC_V, WC_V, WD + 2 * WD_BC, WD, H_D, H_D)

LANE = 128
VMEM_LIMIT = 52 * 1024 * 1024

AB_MAIN = 2 * WA_QK + 2 * WA_V + 4 * WB
CD_MAIN = 2 * WC_QK + 2 * WC_V + (WD + 2 * WD_BC) + WD
TN_PROJ = 512
AB_COLS = -(-(AB_MAIN + LANE) // TN_PROJ) * TN_PROJ
CD_COLS = -(-(CD_MAIN + LANE) // TN_PROJ) * TN_PROJ
AB_RF, AB_RB, AB_BF, AB_BB, AB_AF, AB_AB = 0, 16, 32, 48, 64, 80
CD_DTF, CD_DTB = 0, 32

CHUNK_A = 64
SUB_A = 16
CHUNK_B = 64
CHUNK_C = 256
CHUNK_D = P_D


def _mx(x):
    return x.astype(MXU_DTYPE)


def _mm(a, b):
    return jnp.dot(_mx(a), _mx(b), preferred_element_type=F32)


def _mm_nt(a, b):
    return lax.dot_general(_mx(a), _mx(b), (((1,), (1,)), ((), ())), preferred_element_type=F32)


def _mm_tn(a, b):
    return lax.dot_general(_mx(a), _mx(b), (((0,), (0,)), ((), ())), preferred_element_type=F32)


def _split3(x):
    hi = _mx(x)
    r1 = x - hi.astype(F32)
    mid = _mx(r1)
    lo = _mx(r1 - mid.astype(F32))
    return hi, mid, lo


def _sel_l(m01, x):
    hi, mid, lo = _split3(x)
    m = _mx(m01)
    d = functools.partial(jnp.dot, preferred_element_type=F32)
    return d(m, hi) + d(m, mid) + d(m, lo)


def _sel_r(x, m01):
    hi, mid, lo = _split3(x)
    m = _mx(m01)
    d = functools.partial(jnp.dot, preferred_element_type=F32)
    return d(hi, m) + d(mid, m) + d(lo, m)


def _mm3(a, b):
    a_hi = _mx(a)
    a_lo = _mx(a - a_hi.astype(F32))
    b_hi = _mx(b)
    b_lo = _mx(b - b_hi.astype(F32))
    d = functools.partial(jnp.dot, preferred_element_type=F32)
    return d(a_hi, b_hi) + d(a_hi, b_lo) + d(a_lo, b_hi)


def _silu(x):
    return x * (1.0 / (1.0 + jnp.exp(-x)))


def _sigmoid(x):
    return 1.0 / (1.0 + jnp.exp(-x))


def _softplus(x):
    return jnp.maximum(x, 0.0) + jnp.log1p(jnp.exp(-jnp.abs(x)))


def _log_sigmoid(x):
    return jnp.minimum(x, 0.0) - jnp.log1p(jnp.exp(-jnp.abs(x)))


def _iota2(shape, dim):
    return lax.broadcasted_iota(jnp.int32, shape, dim)


def _f01(mask):
    return jnp.where(mask, 1.0, 0.0).astype(F32)


def _shift_rows(x, prev_row, next_row):
    c = x.shape[0]
    r = _iota2(x.shape, 0)
    xp = jnp.where(r == 0, prev_row, pltpu.roll(x, 1, 0))
    xn = jnp.where(r == c - 1, next_row, pltpu.roll(x, c - 1, 0))
    return xp, xn


def _conv3_chunk(ref, w_ref, start, c, t_total, bias=None):
    x = ref[pl.ds(start, c), :]
    prev = ref[pl.ds(jnp.maximum(start - 1, 0), 1), :] * (start > 0).astype(F32)
    nxt = ref[pl.ds(jnp.minimum(start + c, t_total - 1), 1), :] * (start + c < t_total).astype(F32)
    xp, xn = _shift_rows(x, prev, nxt)
    y = xp * w_ref[0:1, :] + x * w_ref[1:2, :] + xn * w_ref[2:3, :]
    if bias is not None:
        y = y + bias
    return y


def _cparams(sem):
    return pltpu.CompilerParams(dimension_semantics=sem, vmem_limit_bytes=VMEM_LIMIT)


def _ada_kernel(c_ref, w_ref, b_ref, o_ref):
    a = _silu(c_ref[...])
    o_ref[0] = _mm3(a, w_ref[0]) + b_ref[0]


def _ada_mod(cvec16, ada_w, ada_b):
    tn = 512
    n3 = ada_w.shape[-1]
    return pl.pallas_call(
        _ada_kernel,
        out_shape=jax.ShapeDtypeStruct((DEPTH, 16, n3), F32),
        grid=(DEPTH, n3 // tn),
        in_specs=[pl.BlockSpec((16, D_MODEL), lambda l, j: (0, 0)),
                  pl.BlockSpec((1, D_MODEL, tn), lambda l, j: (l, 0, j)),
                  pl.BlockSpec((1, 1, tn), lambda l, j: (l, 0, j))],
        out_specs=pl.BlockSpec((1, 16, tn), lambda l, j: (l, 0, j)),
        compiler_params=_cparams(("arbitrary", "arbitrary")),
        name="ada_mod",
    )(cvec16, ada_w, ada_b.reshape(DEPTH, 1, n3))


def _norm_mod_kernel(x_ref, g_ref, m_ref, o_ref):
    x = x_ref[0]
    y = x * lax.rsqrt(jnp.mean(x * x, axis=-1, keepdims=True) + 1e-6)
    y = y * g_ref[...]
    o_ref[...] = (y * (1.0 + m_ref[0, 1:2, :]) + m_ref[0, 0:1, :]).astype(o_ref.dtype)


def _norm_mod(x, g, mod, shared):
    b, t, d = x.shape
    tt = 256
    nt = t // tt
    mmap = (lambda i, j: (0, 0, 0)) if shared else (lambda i, j: (i, 0, 0))
    return pl.pallas_call(
        _norm_mod_kernel,
        out_shape=jax.ShapeDtypeStruct((b * t, d), MXU_DTYPE),
        grid=(b, nt),
        in_specs=[pl.BlockSpec((1, tt, d), lambda i, j: (i, j, 0)),
                  pl.BlockSpec((1, d), lambda i, j: (0, 0)),
                  pl.BlockSpec((1, 3, d), mmap)],
        out_specs=pl.BlockSpec((tt, d), lambda i, j: (i * nt + j, 0)),
        compiler_params=_cparams(("arbitrary", "arbitrary")),
        name="norm_mod",
    )(x, g.reshape(1, d), mod)


def _proj_kernel(h_ref, w_ref, o_ref):
    o_ref[...] = jnp.dot(h_ref[...], w_ref[...], preferred_element_type=F32)


def _proj(h, w):
    m, d = h.shape
    n = w.shape[1]
    tm = 1024
    return pl.pallas_call(
        _proj_kernel,
        out_shape=jax.ShapeDtypeStruct((m, n), F32),
        grid=(m // tm, n // TN_PROJ),
        in_specs=[pl.BlockSpec((tm, d), lambda i, j: (i, 0)),
                  pl.BlockSpec((d, TN_PROJ), lambda i, j: (0, j))],
        out_specs=pl.BlockSpec((tm, TN_PROJ), lambda i, j: (i, j)),
        compiler_params=_cparams(("arbitrary", "arbitrary")),
        name="in_proj",
    )(h, w)


def _out_kernel(y1_ref, y2_ref, w1_ref, w2_ref, x_ref, m_ref, o_ref):
    acc = jnp.dot(y1_ref[...], w1_ref[...], preferred_element_type=F32)
    acc = acc + jnp.dot(y2_ref[...], w2_ref[...], preferred_element_type=F32)
    o_ref[...] = x_ref[...] + m_ref[0, 2:3, :] * acc


def _out_proj(y1, y2, w1, w2, x2d, mod, t, shared):
    m, d = x2d.shape
    k1, k2 = y1.shape[1], y2.shape[1]
    tm, tn = 1024, 512
    per_b = t // tm if t >= tm else 1
    mmap = (lambda i, j: (0, 0, j)) if shared else (lambda i, j: (i // per_b, 0, j))
    return pl.pallas_call(
        _out_kernel,
        out_shape=jax.ShapeDtypeStruct((m, d), F32),
        grid=(m // tm, d // tn),
        in_specs=[pl.BlockSpec((tm, k1), lambda i, j: (i, 0)),
                  pl.BlockSpec((tm, k2), lambda i, j: (i, 0)),
                  pl.BlockSpec((k1, tn), lambda i, j: (0, j)),
                  pl.BlockSpec((k2, tn), lambda i, j: (0, j)),
                  pl.BlockSpec((tm, tn), lambda i, j: (i, j)),
                  pl.BlockSpec((1, 3, tn), mmap)],
        out_specs=pl.BlockSpec((tm, tn), lambda i, j: (i, j)),
        compiler_params=_cparams(("arbitrary", "arbitrary")),
        name="out_proj",
    )(y1, y2, w1, w2, x2d, mod)


def _final_norm_kernel(x_ref, g_ref, o_ref):
    x = x_ref[...]
    o_ref[...] = x * lax.rsqrt(jnp.mean(x * x, axis=-1, keepdims=True) + 1e-6) * g_ref[...]


def _final_norm(x2d, g):
    m, d = x2d.shape
    tt = 256
    return pl.pallas_call(
        _final_norm_kernel,
        out_shape=jax.ShapeDtypeStruct((m, d), F32),
        grid=(m // tt,),
        in_specs=[pl.BlockSpec((tt, d), lambda i: (i, 0)),
                  pl.BlockSpec((1, d), lambda i: (0, 0))],
        out_specs=pl.BlockSpec((tt, d), lambda i: (i, 0)),
        compiler_params=_cparams(("arbitrary",)),
        name="final_norm",
    )(x2d, g.reshape(1, d))


def _gla_masks(c, s, rev):
    row, col = _iota2((c, c), 0), _iota2((c, c), 1)
    tri = (col >= row) if rev else (col <= row)
    same = (col // s) == (row // s)
    return _f01(tri), _f01(jnp.logical_and(tri, same)), tri


def _gla_chunk(q, k, v, l, st, masks, c, s, rev):
    tri01, blk01, tri = masks
    g = _sel_l(tri01, l)
    cb = _sel_l(blk01, l)
    gl = g[0:1] if rev else g[c - 1:c]
    qt = q * jnp.exp(cb)
    nb = c // s
    rowi = _iota2((c, LANE), 0)
    blocks = []
    for i in range(nb):
        if rev:
            g_ref = g[(i + 1) * s:(i + 1) * s + 1] if i < nb - 1 else jnp.zeros((1, LANE), F32)
            valid = rowi >= i * s
        else:
            g_ref = g[i * s - 1:i * s] if i > 0 else jnp.zeros((1, LANE), F32)
            valid = rowi < (i + 1) * s
        kh = k * jnp.exp(jnp.where(valid, g_ref - g, -1e30))
        blocks.append(_mm_nt(qt[i * s:(i + 1) * s], kh))
    p = jnp.where(tri, jnp.concatenate(blocks, axis=0), 0.0)
    o = _mm(p, v) + _mm_nt(q * jnp.exp(g), st)
    st_new = st * jnp.exp(gl) + _mm_tn(v, k * jnp.exp(gl - g))
    return o, st_new


def _gla_kernel(q_ref, k_ref, v_ref, z_ref, sm_ref, wg_ref, bg_ref, ng_ref, s0f_ref, s0b_ref,
                y_ref, sf_ref, sb_ref, of_scr, ob_scr, *, t, c, s):
    n = t // c
    mf = _gla_masks(c, s, False)
    mb = _gla_masks(c, s, True)
    wg_f, wg_b = wg_ref[0], wg_ref[1]
    bg_f, bg_b = bg_ref[0], bg_ref[1]

    def gate(start, wg, bg):
        logit = _mm(sm_ref[pl.ds(start, c), :], wg) + bg
        return _log_sigmoid(logit) * (1.0 / GLA_GATE_NORM)

    def body(i, carry):
        st_f, st_b = carry
        a = pl.multiple_of(i * c, c)
        b = pl.multiple_of((n - 1 - i) * c, c)
        o_f, st_f = _gla_chunk(q_ref[pl.ds(a, c), :] * (K_A ** -0.5), k_ref[pl.ds(a, c), :],
                               v_ref[pl.ds(a, c), :], gate(a, wg_f, bg_f), st_f, mf, c, s, False)
        of_scr[pl.ds(a, c), :] = o_f
        o_b, st_b = _gla_chunk(q_ref[pl.ds(b, c), :] * (K_A ** -0.5), k_ref[pl.ds(b, c), :],
                               v_ref[pl.ds(b, c), :], gate(b, wg_b, bg_b), st_b, mb, c, s, True)
        ob_scr[pl.ds(b, c), :] = o_b
        return st_f, st_b

    st_f, st_b = lax.fori_loop(0, n, body, (s0f_ref[0, 0], s0b_ref[0, 0]))
    sf_ref[0, 0] = st_f
    sb_ref[0, 0] = st_b

    def epi(i, _):
        a = pl.multiple_of(i * c, c)
        o = of_scr[pl.ds(a, c), :] + ob_scr[pl.ds(a, c), :]
        y = o * lax.rsqrt(jnp.mean(o * o, axis=-1, keepdims=True) + 1e-6) * ng_ref[...]
        y_ref[pl.ds(a, c), :] = (y * _silu(z_ref[pl.ds(a, c), :])).astype(y_ref.dtype)
        return 0

    lax.fori_loop(0, n, epi, 0)


def _gla(p, b, t, wg, bg, ng, s0f, s0b):
    c, s = CHUNK_A, SUB_A
    kq, kv = K_A // LANE, V_A // LANE
    q_off, k_off = 0, WA_QK // K_A
    v_off, z_off = (2 * WA_QK) // V_A, (2 * WA_QK + WA_V) // V_A
    sm_blk = AB_MAIN // LANE
    st_shape = jax.ShapeDtypeStruct((b, H_A, V_A, K_A), F32)
    st_spec = pl.BlockSpec((1, 1, V_A, K_A), lambda i, h: (i, h, 0, 0))
    return pl.pallas_call(
        functools.partial(_gla_kernel, t=t, c=c, s=s),
        out_shape=(jax.ShapeDtypeStruct((b * t, WA_V), MXU_DTYPE), st_shape, st_shape),
        grid=(b, H_A),
        in_specs=[pl.BlockSpec((t, K_A), lambda i, h: (i, q_off + h)),
                  pl.BlockSpec((t, K_A), lambda i, h: (i, k_off + h)),
                  pl.BlockSpec((t, V_A), lambda i, h: (i, v_off + h)),
                  pl.BlockSpec((t, V_A), lambda i, h: (i, z_off + h)),
                  pl.BlockSpec((t, LANE), lambda i, h: (i, sm_blk)),
                  pl.BlockSpec((2, LANE, K_A), lambda i, h: (0, 0, h)),
                  pl.BlockSpec((2, 1, K_A), lambda i, h: (0, 0, h)),
                  pl.BlockSpec((1, V_A), lambda i, h: (0, 0)),
                  st_spec, st_spec],
        out_specs=(pl.BlockSpec((t, V_A), lambda i, h: (i, h)), st_spec, st_spec),
        scratch_shapes=[pltpu.VMEM((t, V_A), F32), pltpu.VMEM((t, V_A), F32)],
        compiler_params=_cparams(("arbitrary", "arbitrary")),
        name="gla_scan",
    )(p, p, p, p, p, wg, bg, ng, s0f, s0b)


def _tri_inverse(a, c):
    nmat = -a
    eye = _f01(_iota2((c, c), 0) == _iota2((c, c), 1))
    x = eye + nmat
    pw = nmat
    for _ in range(int(math.log2(c)) - 1):
        pw = _mm3(pw, pw)
        x = x + _mm3(x, pw)
    return x


def _gdn_chunk(q, k, v, beta, la, st, c, rev):
    row, col = _iota2((c, c), 0), _iota2((c, c), 1)
    if rev:
        a01, bm, incl, strict = _f01(col >= row), _f01(row < col), col >= row, col > row
    else:
        a01, bm, incl, strict = _f01(col <= row), _f01(row > col), col <= row, col < row
    delta = _sel_l(a01, la[:, :c] * bm)
    dec = jnp.where(incl, jnp.exp(delta), 0.0)
    g = _sel_l(a01, la)
    eg = jnp.exp(g)
    gl = g[0:1] if rev else g[c - 1:c]
    kb = k * beta
    a = jnp.where(strict, _mm_nt(kb, k) * dec, 0.0)
    tinv = _tri_inverse(a, c)
    u = _mm(tinv, v * beta)
    w = _mm(tinv, kb * eg)
    v_new = u - _mm(w, st)
    attn = _mm_nt(q, k) * dec
    o = _mm(q * eg, st) + _mm(attn, v_new)
    st_new = st * jnp.exp(gl) + _mm_tn(k * jnp.exp(gl - g), v_new)
    return o, st_new


def _l2n(x):
    return x * lax.rsqrt(jnp.sum(x * x, axis=-1, keepdims=True) + 1e-6)


def _gdn_kernel(q_ref, k_ref, v_ref, z_ref, sm_ref, wq_ref, wk_ref, wv_ref, dtb_ref, alog_ref, ng_ref,
                s0f_ref, s0b_ref, y_ref, sf_ref, sb_ref, qs, ks, vs, gs, of_scr, ob_scr, *, t, c):
    n = t // c
    h = pl.program_id(1)
    lane = _iota2((1, LANE), 1)
    is_beta = jnp.logical_and(lane >= AB_BF, lane < AB_AF)
    neg_a = -jnp.exp(alog_ref[...])
    srow, scol = _iota2((LANE, 4 * LANE), 0), _iota2((LANE, 4 * LANE), 1)
    base = jnp.where(scol < LANE, AB_BF, jnp.where(scol < 2 * LANE, AB_BB,
                                                   jnp.where(scol < 3 * LANE, AB_AF, AB_AB)))
    sel = _f01(srow == base + h)

    def prep(i, _):
        a = pl.multiple_of(i * c, c)
        qc = _silu(_conv3_chunk(q_ref, wq_ref, a, c, t))
        kc = _silu(_conv3_chunk(k_ref, wk_ref, a, c, t))
        vc = _silu(_conv3_chunk(v_ref, wv_ref, a, c, t))
        qs[pl.ds(a, c), :] = _l2n(qc) * (K_B ** -0.5)
        ks[pl.ds(a, c), :] = _l2n(kc)
        vs[pl.ds(a, c), :] = vc
        sm = sm_ref[pl.ds(a, c), :]
        gt = jnp.where(is_beta, _sigmoid(sm), neg_a * _softplus(sm + dtb_ref[...]))
        gs[pl.ds(a, c), :] = _sel_r(gt, sel)
        return 0

    lax.fori_loop(0, n, prep, 0)

    def body(i, carry):
        st_f, st_b = carry
        a = pl.multiple_of(i * c, c)
        b = pl.multiple_of((n - 1 - i) * c, c)
        o_f, st_f = _gdn_chunk(qs[pl.ds(a, c), :], ks[pl.ds(a, c), :], vs[pl.ds(a, c), :],
                               gs[pl.ds(a, c), 0:LANE], gs[pl.ds(a, c), 2 * LANE:3 * LANE], st_f, c, False)
        of_scr[pl.ds(a, c), :] = o_f
        o_b, st_b = _gdn_chunk(qs[pl.ds(b, c), :], ks[pl.ds(b, c), :], vs[pl.ds(b, c), :],
                               gs[pl.ds(b, c), LANE:2 * LANE], gs[pl.ds(b, c), 3 * LANE:4 * LANE], st_b, c, True)
        ob_scr[pl.ds(b, c), :] = o_b
        return st_f, st_b

    st_f, st_b = lax.fori_loop(0, n, body, (s0f_ref[0, 0], s0b_ref[0, 0]))
    sf_ref[0, 0] = st_f
    sb_ref[0, 0] = st_b

    def epi(i, _):
        a = pl.multiple_of(i * c, c)
        o = of_scr[pl.ds(a, c), :] + ob_scr[pl.ds(a, c), :]
        y = o * lax.rsqrt(jnp.mean(o * o, axis=-1, keepdims=True) + 1e-6) * ng_ref[...]
        y_ref[pl.ds(a, c), :] = (y * _silu(z_ref[pl.ds(a, c), :])).astype(y_ref.dtype)
        return 0

    lax.fori_loop(0, n, epi, 0)


def _gdn(p, b, t, conv_w, dtb_row, alog_row, ng, s0f, s0b):
    c = CHUNK_B
    base = (2 * WA_QK + 2 * WA_V) // K_B
    q_off, k_off, v_off, z_off = base, base + H_B, base + 2 * H_B, base + 3 * H_B
    sm_blk = AB_MAIN // LANE
    st_shape = jax.ShapeDtypeStruct((b, H_B, K_B, V_B), F32)
    st_spec = pl.BlockSpec((1, 1, K_B, V_B), lambda i, h: (i, h, 0, 0))
    row_spec = pl.BlockSpec((1, LANE), lambda i, h: (0, 0))
    return pl.pallas_call(
        functools.partial(_gdn_kernel, t=t, c=c),
        out_shape=(jax.ShapeDtypeStruct((b * t, WB), MXU_DTYPE), st_shape, st_shape),
        grid=(b, H_B),
        in_specs=[pl.BlockSpec((t, K_B), lambda i, h: (i, q_off + h)),
                  pl.BlockSpec((t, K_B), lambda i, h: (i, k_off + h)),
                  pl.BlockSpec((t, V_B), lambda i, h: (i, v_off + h)),
                  pl.BlockSpec((t, V_B), lambda i, h: (i, z_off + h)),
                  pl.BlockSpec((t, LANE), lambda i, h: (i, sm_blk)),
                  pl.BlockSpec((3, K_B), lambda i, h: (0, h)),
                  pl.BlockSpec((3, K_B), lambda i, h: (0, H_B + h)),
                  pl.BlockSpec((3, V_B), lambda i, h: (0, 2 * H_B + h)),
                  row_spec, row_spec, row_spec, st_spec, st_spec],
        out_specs=(pl.BlockSpec((t, V_B), lambda i, h: (i, h)), st_spec, st_spec),
        scratch_shapes=[pltpu.VMEM((t, K_B), F32), pltpu.VMEM((t, K_B), F32), pltpu.VMEM((t, V_B), F32),
                        pltpu.VMEM((t, 4 * LANE), F32), pltpu.VMEM((t, V_B), F32), pltpu.VMEM((t, V_B), F32)],
        compiler_params=_cparams(("arbitrary", "arbitrary")),
        name="gdn_scan",
    )(p, p, p, p, p, conv_w, conv_w, conv_w, dtb_row, alog_row, ng, s0f, s0b)


def _rotary(x, cos, sin_signed):
    even = (_iota2(x.shape, 1) % 2) == 0
    swapped = jnp.where(even, pltpu.roll(x, LANE - 1, 1), pltpu.roll(x, 1, 1))
    return x * cos + swapped * sin_signed


def _ret_kernel(q_ref, k_ref, v_ref, z_ref, cos_ref, sin_ref, ng_ref, nb_ref, s0f_ref, s0b_ref,
                y_ref, sf_ref, sb_ref, of_scr, ob_scr, *, t, c, on_grid):
    n = t // c
    h = pl.program_id(1)
    hf = jnp.full((1, 1), h, jnp.int32).astype(F32)
    lg_f = jnp.log1p(-jnp.exp2(-5.0 - hf))
    lg_b = jnp.log1p(-jnp.exp2(-5.0 - (H_C - 1.0 - hf)))
    row, col = _iota2((c, c), 0), _iota2((c, c), 1)
    dmat_f = jnp.where(col <= row, jnp.exp((row - col).astype(F32) * lg_f), 0.0)
    dmat_b = jnp.where(col >= row, jnp.exp((col - row).astype(F32) * lg_b), 0.0)
    r1 = _iota2((c, 1), 0).astype(F32)
    eg_f, ed_f = jnp.exp((r1 + 1.0) * lg_f), jnp.exp((c - 1.0 - r1) * lg_f)
    eg_b, ed_b = jnp.exp((c - r1) * lg_b), jnp.exp(r1 * lg_b)
    tot_f, tot_b = jnp.exp(c * lg_f), jnp.exp(c * lg_b)

    def load_qk(start):
        q = q_ref[pl.ds(start, c), :]
        k = k_ref[pl.ds(start, c), :] * (K_C ** -0.5)
        if on_grid:
            cs, sn = cos_ref[pl.ds(start, c), :], sin_ref[pl.ds(start, c), :]
            q, k = _rotary(q, cs, sn), _rotary(k, cs, sn)
        return q, k

    def chunk(start, st, dmat, eg, ed, tot):
        q, k = load_qk(start)
        v = v_ref[pl.ds(start, c), :]
        o = _mm(_mm_nt(q, k) * dmat, v) + _mm(q * eg, st)
        return o, st * tot + _mm_tn(k * ed, v)

    def body(i, carry):
        st_f, st_b = carry
        a = pl.multiple_of(i * c, c)
        b = pl.multiple_of((n - 1 - i) * c, c)
        o_f, st_f = chunk(a, st_f, dmat_f, eg_f, ed_f, tot_f)
        of_scr[pl.ds(a, c), :] = o_f
        o_b, st_b = chunk(b, st_b, dmat_b, eg_b, ed_b, tot_b)
        ob_scr[pl.ds(b, c), :] = o_b
        return st_f, st_b

    st_f, st_b = lax.fori_loop(0, n, body, (s0f_ref[0, 0], s0b_ref[0, 0]))
    sf_ref[0, 0] = st_f
    sb_ref[0, 0] = st_b

    def epi(i, _):
        a = pl.multiple_of(i * c, c)
        o = of_scr[pl.ds(a, c), :] + ob_scr[pl.ds(a, c), :]
        mu = jnp.mean(o, axis=-1, keepdims=True)
        d = o - mu
        var = jnp.mean(d * d, axis=-1, keepdims=True)
        y = d * lax.rsqrt(var + 1e-5) * ng_ref[...] + nb_ref[...]
        y_ref[pl.ds(a, c), :] = (y * _silu(z_ref[pl.ds(a, c), :])).astype(y_ref.dtype)
        return 0

    lax.fori_loop(0, n, epi, 0)


def _ret(p, b, t, cos_t, sin_t, ng, nb, s0f, s0b, on_grid):
    c = min(CHUNK_C, t)
    q_off, k_off = 0, WC_QK // K_C
    v_off, z_off = (2 * WC_QK) // V_C, (2 * WC_QK + WC_V) // V_C
    st_shape = jax.ShapeDtypeStruct((b, H_C, K_C, V_C), F32)
    st_spec = pl.BlockSpec((1, 1, K_C, V_C), lambda i, h: (i, h, 0, 0))
    tab_spec = pl.BlockSpec((t, K_C), lambda i, h: (0, 0))
    vrow = pl.BlockSpec((1, V_C), lambda i, h: (0, 0))
    return pl.pallas_call(
        functools.partial(_ret_kernel, t=t, c=c, on_grid=on_grid),
        out_shape=(jax.ShapeDtypeStruct((b * t, WC_V), MXU_DTYPE), st_shape, st_shape),
        grid=(b, H_C),
        in_specs=[pl.BlockSpec((t, K_C), lambda i, h: (i, q_off + h)),
                  pl.BlockSpec((t, K_C), lambda i, h: (i, k_off + h)),
                  pl.BlockSpec((t, V_C), lambda i, h: (i, v_off + h)),
                  pl.BlockSpec((t, V_C), lambda i, h: (i, z_off + h)),
                  tab_spec, tab_spec, vrow, vrow, st_spec, st_spec],
        out_specs=(pl.BlockSpec((t, V_C), lambda i, h: (i, h)), st_spec, st_spec),
        scratch_shapes=[pltpu.VMEM((t, V_C), F32), pltpu.VMEM((t, V_C), F32)],
        compiler_params=_cparams(("arbitrary", "arbitrary")),
        name="ret_scan",
    )(p, p, p, p, cos_t, sin_t, ng, nb, s0f, s0b)


def _ssd_chunk(cm, bd, v, la_x, st, consts, c, rev):
    a01, bm_t, mask_t, bd_mask = consts
    hp = HEADS_PER_GROUP
    delta = _sel_l(a01, la_x * bm_t)
    dec = jnp.where(mask_t, jnp.exp(delta), 0.0)
    gx = _sel_l(a01, la_x)
    gl = gx[0:1] if rev else gx[c - 1:c]
    scores = _mm_nt(cm, jnp.concatenate([bd] * hp, axis=0)) * dec
    v_bd = jnp.concatenate([v] * hp, axis=0) * bd_mask
    o = _mm(scores, v_bd) + _mm(cm, st) * jnp.exp(gx)
    st_new = st * jnp.exp(gl) + _mm_tn(bd, v * jnp.exp(gl - gx))
    return o, st_new


def _ssd_kernel(x_ref, b_ref, c_ref, z_ref, sm_ref, wx_ref, wb_ref, wc_ref, bx_ref, bb_ref, bc_ref,
                dtb_ref, alog_ref, dsk_ref, ng_ref, s0f_ref, s0b_ref,
                y_ref, sf_ref, sb_ref, xs, bs, cs, of_scr, ob_scr, *, t, c):
    n = t // c
    hp = HEADS_PER_GROUP
    w = hp * P_D
    grp = pl.program_id(1)
    neg_a = -jnp.exp(alog_ref[...])
    erow, ecol = _iota2((LANE, w), 0), _iota2((LANE, w), 1)
    expand_f = _f01(erow == CD_DTF + grp * hp + ecol // P_D)
    expand_b = _f01(erow == CD_DTB + grp * hp + ecol // P_D)

    def prep(i, _):
        a = pl.multiple_of(i * c, c)
        xs[pl.ds(a, c), :] = _silu(_conv3_chunk(x_ref, wx_ref, a, c, t, bx_ref[...]))
        bs[pl.ds(a, c), :] = _silu(_conv3_chunk(b_ref, wb_ref, a, c, t, bb_ref[...]))
        cs[pl.ds(a, c), :] = _silu(_conv3_chunk(c_ref, wc_ref, a, c, t, bc_ref[...]))
        return 0

    lax.fori_loop(0, n, prep, 0)

    def gates(start, expand):
        dt = _softplus(sm_ref[pl.ds(start, c), :] + dtb_ref[...])
        return _sel_r(dt, expand), _sel_r(neg_a * dt, expand)

    row, col = _iota2((c, c), 0), _iota2((c, c), 1)
    rt, ct = _iota2((c, hp * c), 0), _iota2((c, hp * c), 1) % c
    brow, bcol = _iota2((hp * c, w), 0), _iota2((hp * c, w), 1)
    bd_mask = _f01((brow // c) == (bcol // P_D))
    cf = (_f01(col <= row), _f01(rt > ct), ct <= rt, bd_mask)
    cb = (_f01(col >= row), _f01(rt < ct), ct >= rt, bd_mask)

    def body(i, carry):
        st_f, st_b = carry
        a = pl.multiple_of(i * c, c)
        b = pl.multiple_of((n - 1 - i) * c, c)
        dt_f, la_f = gates(a, expand_f)
        o_f, st_f = _ssd_chunk(cs[pl.ds(a, c), :], bs[pl.ds(a, c), :],
                               xs[pl.ds(a, c), :] * dt_f, la_f, st_f, cf, c, False)
        of_scr[pl.ds(a, c), :] = o_f
        dt_b, la_b = gates(b, expand_b)
        o_b, st_b = _ssd_chunk(cs[pl.ds(b, c), :], bs[pl.ds(b, c), :],
                               xs[pl.ds(b, c), :] * dt_b, la_b, st_b, cb, c, True)
        ob_scr[pl.ds(b, c), :] = o_b
        return st_f, st_b

    st_f, st_b = lax.fori_loop(0, n, body, (s0f_ref[0, 0], s0b_ref[0, 0]))
    sf_ref[0, 0] = st_f
    sb_ref[0, 0] = st_b

    def epi(i, _):
        a = pl.multiple_of(i * c, c)
        o = of_scr[pl.ds(a, c), :] + ob_scr[pl.ds(a, c), :] + dsk_ref[...] * xs[pl.ds(a, c), :]
        o = o * _silu(z_ref[pl.ds(a, c), :])
        y = o * lax.rsqrt(jnp.mean(o * o, axis=-1, keepdims=True) + 1e-6) * ng_ref[...]
        y_ref[pl.ds(a, c), :] = y.astype(y_ref.dtype)
        return 0

    lax.fori_loop(0, n, epi, 0)


def _ssd(p, b, t, conv_w, conv_b, dtb_row, alog_row, dskip, ng, s0f, s0b):
    c = CHUNK_D
    w = HEADS_PER_GROUP * P_D
    base = 2 * WC_QK + 2 * WC_V
    x_off = base // w
    b_off = (base + WD) // N_D
    c_off = (base + WD + WD_BC) // N_D
    z_off = (base + WD + 2 * WD_BC) // w
    sm_blk = CD_MAIN // LANE
    st_shape = jax.ShapeDtypeStruct((b, G_D, N_D, w), F32)
    st_spec = pl.BlockSpec((1, 1, N_D, w), lambda i, g: (i, g, 0, 0))
    row_spec = pl.BlockSpec((1, LANE), lambda i, g: (0, 0))
    return pl.pallas_call(
        functools.partial(_ssd_kernel, t=t, c=c),
        out_shape=(jax.ShapeDtypeStruct((b * t, WD), MXU_DTYPE), st_shape, st_shape),
        grid=(b, G_D),
        in_specs=[pl.BlockSpec((t, w), lambda i, g: (i, x_off + g)),
                  pl.BlockSpec((t, N_D), lambda i, g: (i, b_off + g)),
                  pl.BlockSpec((t, N_D), lambda i, g: (i, c_off + g)),
                  pl.BlockSpec((t, w), lambda i, g: (i, z_off + g)),
                  pl.BlockSpec((t, LANE), lambda i, g: (i, sm_blk)),
                  pl.BlockSpec((3, w), lambda i, g: (0, g)),
                  pl.BlockSpec((3, N_D), lambda i, g: (0, WD // N_D + g)),
                  pl.BlockSpec((3, N_D), lambda i, g: (0, (WD + WD_BC) // N_D + g)),
                  pl.BlockSpec((1, w), lambda i, g: (0, g)),
                  pl.BlockSpec((1, N_D), lambda i, g: (0, WD // N_D + g)),
                  pl.BlockSpec((1, N_D), lambda i, g: (0, (WD + WD_BC) // N_D + g)),
                  row_spec, row_spec,
                  pl.BlockSpec((1, w), lambda i, g: (0, g)),
                  pl.BlockSpec((1, w), lambda i, g: (0, g)),
                  st_spec, st_spec],
        out_specs=(pl.BlockSpec((t, w), lambda i, g: (i, g)), st_spec, st_spec),
        scratch_shapes=[pltpu.VMEM((t, w), F32), pltpu.VMEM((t, N_D), F32), pltpu.VMEM((t, N_D), F32),
                        pltpu.VMEM((t, w), F32), pltpu.VMEM((t, w), F32)],
        compiler_params=_cparams(("arbitrary", "arbitrary")),
        name="ssd_scan",
    )(p, p, p, p, p, conv_w, conv_w, conv_w, conv_b, conv_b, conv_b, dtb_row, alog_row, dskip, ng, s0f, s0b)


def _split_cols(w, sizes):
    out, o = [], 0
    for s in sizes:
        out.append(w[..., o:o + s])
        o += s
    return out


def _pad_cols(w, n):
    return jnp.pad(w, [(0, 0)] * (w.ndim - 1) + [(0, n - w.shape[-1])])


def _lane_row(pieces):
    row = jnp.zeros((LANE,), F32)
    for off, vec in pieces:
        row = lax.dynamic_update_slice(row, vec.astype(F32), (off,))
    return row.reshape(1, LANE)


def _rotary_tables(t):
    n_rows = t // GRID_W
    rows = jnp.repeat(jnp.arange(n_rows), GRID_W).astype(F32)
    cols = jnp.tile(jnp.arange(GRID_W), n_rows).astype(F32)
    n_freq = K_C // 4
    inv_freq = ROPE_BASE ** (-jnp.arange(n_freq, dtype=F32) / n_freq)
    ang = jnp.concatenate([rows[:, None] * inv_freq, cols[:, None] * inv_freq], axis=-1)
    cos = jnp.repeat(jnp.cos(ang), 2, axis=-1)
    sin = jnp.repeat(jnp.sin(ang), 2, axis=-1)
    sign = jnp.tile(jnp.array([-1.0, 1.0], F32), K_C // 2)
    return cos, sin * sign


def kernel(x_prompt, x_sample, state_gla_fwd, state_gla_bwd, state_gdn_fwd, state_gdn_bwd, state_ret_fwd, state_ret_bwd, state_ssd_fwd, state_ssd_bwd, c, c_ctx, ada_w, ada_b, norm_g, final_norm_g, ab_w_in, ab_w_out, gla_gate_w2, gla_gate_b, gla_norm_g, gdn_conv_w, gdn_a_log, gdn_dt_bias, gdn_norm_g, cd_w_in, cd_w_out, ret_norm_g, ret_norm_b, ssd_conv_w, ssd_conv_b, ssd_a_log, ssd_dt_bias, ssd_d, ssd_norm_g):
    bp, tp, _ = x_prompt.shape
    bs, ts, _ = x_sample.shape
    hp = HEADS_PER_GROUP

    cvec = jnp.concatenate([c_ctx[None, :], c], axis=0)
    cvec16 = jnp.pad(cvec, ((0, 16 - cvec.shape[0]), (0, 0)))
    mod = _ada_mod(cvec16, ada_w, ada_b).reshape(DEPTH, 16, 3, D_MODEL)

    groups = [
        dict(x=x_prompt, b=bp, t=tp, shared=True, on_grid=False, rows=slice(0, 1)),
        dict(x=x_sample, b=bs, t=ts, shared=False, on_grid=True, rows=slice(1, 1 + bs)),
    ]
    cos_t, sin_t = _rotary_tables(ts)
    xs = [g["x"].reshape(g["b"] * g["t"], D_MODEL) for g in groups]
    new_states = {}

    for l in range(DEPTH):
        i = l // 2
        if l % 2 == 0:
            qa, ka, va, za, ra_f, ra_b, qkv_b, zb, be_f, be_b, a_f, a_b = _split_cols(ab_w_in[i], AB_SIZES)
            w_in = _pad_cols(jnp.concatenate(
                [qa, ka, va, za, qkv_b, zb, ra_f, ra_b, be_f, be_b, a_f, a_b], axis=1), AB_COLS).astype(MXU_DTYPE)
            w_o1 = ab_w_out[i][:WA_V].astype(MXU_DTYPE)
            w_o2 = ab_w_out[i][WA_V:].astype(MXU_DTYPE)
            wg = jnp.zeros((2, LANE, WA_QK), F32)
            wg = wg.at[0, AB_RF:AB_RF + GLA_RANK].set(gla_gate_w2[i, 0])
            wg = wg.at[1, AB_RB:AB_RB + GLA_RANK].set(gla_gate_w2[i, 1])
            bg = gla_gate_b[i].reshape(2, 1, WA_QK)
            dtb_row = _lane_row([(AB_AF, gdn_dt_bias[i, 0]), (AB_AB, gdn_dt_bias[i, 1])])
            alog_row = _lane_row([(AB_AF, gdn_a_log[i, 0]), (AB_AB, gdn_a_log[i, 1])])
        else:
            w_in = _pad_cols(cd_w_in[i], CD_COLS).astype(MXU_DTYPE)
            w_o1 = cd_w_out[i][:WC_V].astype(MXU_DTYPE)
            w_o2 = cd_w_out[i][WC_V:].astype(MXU_DTYPE)
            dtb_row = _lane_row([(CD_DTF, ssd_dt_bias[i, 0]), (CD_DTB, ssd_dt_bias[i, 1])])
            alog_row = _lane_row([(CD_DTF, ssd_a_log[i, 0]), (CD_DTB, ssd_a_log[i, 1])])
            dskip = jnp.repeat(ssd_d[i], P_D).reshape(1, WD)

        for gi, g in enumerate(groups):
            b, t = g["b"], g["t"]
            mod_g = mod[l, g["rows"]]
            h = _norm_mod(xs[gi].reshape(b, t, D_MODEL), norm_g[l], mod_g, g["shared"])
            p = _proj(h, w_in)
            if l % 2 == 0:
                if gi == 0:
                    sa_f = sa_b = jnp.zeros((b, H_A, V_A, K_A), F32)
                    sb_f = sb_b = jnp.zeros((b, H_B, K_B, V_B), F32)
                else:
                    sa_f = jnp.swapaxes(state_gla_fwd[:, i], -1, -2)
                    sa_b = jnp.swapaxes(state_gla_bwd[:, i], -1, -2)
                    sb_f, sb_b = state_gdn_fwd[:, i], state_gdn_bwd[:, i]
                y1, fa_f, fa_b = _gla(p, b, t, wg, bg, gla_norm_g[i].reshape(1, V_A), sa_f, sa_b)
                y2, fb_f, fb_b = _gdn(p, b, t, gdn_conv_w[i], dtb_row, alog_row,
                                      gdn_norm_g[i].reshape(1, V_B), sb_f, sb_b)
                if gi == 0:
                    new_states["gla_f"] = jnp.swapaxes(fa_f, -1, -2)
                    new_states["gla_b"] = jnp.swapaxes(fa_b, -1, -2)
                    new_states["gdn_f"], new_states["gdn_b"] = fb_f, fb_b
            else:
                if gi == 0:
                    sc_f = sc_b = jnp.zeros((b, H_C, K_C, V_C), F32)
                    sd_f = sd_b = jnp.zeros((b, G_D, N_D, hp * P_D), F32)
                else:
                    sc_f, sc_b = state_ret_fwd[:, i], state_ret_bwd[:, i]

                    def to_grp(s):
                        s = s.reshape(b, G_D, hp, N_D, P_D)
                        return jnp.transpose(s, (0, 1, 3, 2, 4)).reshape(b, G_D, N_D, hp * P_D)

                    sd_f, sd_b = to_grp(state_ssd_fwd[:, i]), to_grp(state_ssd_bwd[:, i])
                tab_c, tab_s = (cos_t, sin_t) if g["on_grid"] else (cos_t[:t], sin_t[:t])
                y1, fc_f, fc_b = _ret(p, b, t, tab_c, tab_s, ret_norm_g[i].reshape(1, V_C),
                                      ret_norm_b[i].reshape(1, V_C), sc_f, sc_b, g["on_grid"])
                y2, fd_f, fd_b = _ssd(p, b, t, ssd_conv_w[i], ssd_conv_b[i].reshape(1, -1), dtb_row, alog_row,
                                      dskip, ssd_norm_g[i].reshape(1, WD), sd_f, sd_b)
                if gi == 0:
                    def from_grp(s):
                        s = s.reshape(b, G_D, N_D, hp, P_D)
                        return jnp.transpose(s, (0, 1, 3, 2, 4)).reshape(b, H_D, N_D, P_D)

                    new_states["ret_f"], new_states["ret_b"] = fc_f, fc_b
                    new_states["ssd_f"], new_states["ssd_b"] = from_grp(fd_f), from_grp(fd_b)
            xs[gi] = _out_proj(y1, y2, w_o1, w_o2, xs[gi], mod_g, t, g["shared"])

    y_prompt = _final_norm(xs[0], final_norm_g).reshape(bp, tp, D_MODEL)
    y_sample = _final_norm(xs[1], final_norm_g).reshape(bs, ts, D_MODEL)
    st = lambda k: new_states[k][:, None]
    return (y_prompt, y_sample, st("gla_f"), st("gla_b"), st("gdn_f"), st("gdn_b"),
            st("ret_f"), st("ret_b"), st("ssd_f"), st("ssd_b"))
```

```python
import functools
import math

import jax
import jax.numpy as jnp
from jax import lax
from jax.experimental import pallas as pl
from jax.experimental.pallas import tpu as pltpu

F32 = jnp.float32
MXU_DTYPE = jnp.bfloat16

D_MODEL = 4096
DEPTH = 2
GRID_W = 64
ROPE_BASE = 10000.0
H_A, K_A, V_A = 8, 128, 256
GLA_RANK = 16
GLA_GATE_NORM = 16.0
H_B, K_B, V_B = 16, 128, 128
H_C, K_C, V_C = 8, 128, 256
H_D, N_D, P_D, G_D = 32, 128, 64, 4
HEADS_PER_GROUP = H_D // G_D
WA_QK, WA_V = H_A * K_A, H_A * V_A
WB = H_B * K_B
WC_QK, WC_V = H_C * K_C, H_C * V_C
WD, WD_BC = H_D * P_D, G_D * N_D
AB_SIZES = (WA_QK, WA_QK, WA_V, WA_V, GLA_RANK, GLA_RANK, 3 * WB, WB, H_B, H_B, H_B, H_B)
CD_SIZES = (WC_QK, WC_QK, WC_V, WC_V, WD + 2 * WD_BC, WD, H_D, H_D)

LANE = 128
VMEM_LIMIT = 52 * 1024 * 1024

AB_MAIN = 2 * WA_QK + 2 * WA_V + 4 * WB
CD_MAIN = 2 * WC_QK + 2 * WC_V + (WD + 2 * WD_BC) + WD
TN_PROJ = 512
AB_COLS = -(-(AB_MAIN + LANE) // TN_PROJ) * TN_PROJ
CD_COLS = -(-(CD_MAIN + LANE) // TN_PROJ) * TN_PROJ
AB_RF, AB_RB, AB_BF, AB_BB, AB_AF, AB_AB = 0, 16, 32, 48, 64, 80
CD_DTF, CD_DTB = 0, 32

CHUNK_A = 64
SUB_A = 16
HEADS_A = 2
CHUNK_B = 128
HEADS_B = 2
PREP_CHUNKS_B = 2
INV_BASE = 8
CHUNK_C = 256
CHUNK_D = P_D
SSD_CHUNKS = 2


def _mx(x):
    return x.astype(MXU_DTYPE)


def _mm(a, b):
    return jnp.dot(_mx(a), _mx(b), preferred_element_type=F32)


def _mm_nt(a, b):
    return lax.dot_general(_mx(a), _mx(b), (((1,), (1,)), ((), ())), preferred_element_type=F32)


def _mm_tn(a, b):
    return lax.dot_general(_mx(a), _mx(b), (((0,), (0,)), ((), ())), preferred_element_type=F32)


def _split3(x):
    hi = _mx(x)
    r1 = x - hi.astype(F32)
    mid = _mx(r1)
    lo = _mx(r1 - mid.astype(F32))
    return hi, mid, lo


def _sel_l(m01, x):
    hi, mid, lo = _split3(x)
    m = _mx(m01)
    d = functools.partial(jnp.dot, preferred_element_type=F32)
    return d(m, hi) + d(m, mid) + d(m, lo)


def _sel_r(x, m01):
    hi, mid, lo = _split3(x)
    m = _mx(m01)
    d = functools.partial(jnp.dot, preferred_element_type=F32)
    return d(hi, m) + d(mid, m) + d(lo, m)


def _sel2_l(m01, x):
    hi = _mx(x)
    lo = _mx(x - hi.astype(F32))
    m = _mx(m01)
    return jnp.dot(m, hi, preferred_element_type=F32) + jnp.dot(m, lo, preferred_element_type=F32)


def _sel2_r(x, m01):
    hi = _mx(x)
    lo = _mx(x - hi.astype(F32))
    m = _mx(m01)
    return jnp.dot(hi, m, preferred_element_type=F32) + jnp.dot(lo, m, preferred_element_type=F32)


def _mm3(a, b):
    a_hi = _mx(a)
    a_lo = _mx(a - a_hi.astype(F32))
    b_hi = _mx(b)
    b_lo = _mx(b - b_hi.astype(F32))
    d = functools.partial(jnp.dot, preferred_element_type=F32)
    return d(a_hi, b_hi) + d(a_hi, b_lo) + d(a_lo, b_hi)


def _silu(x):
    return x * (1.0 / (1.0 + jnp.exp(-x)))


def _sigmoid(x):
    return 1.0 / (1.0 + jnp.exp(-x))


def _softplus(x):
    return jnp.maximum(x, 0.0) + jnp.log1p(jnp.exp(-jnp.abs(x)))


def _log_sigmoid(x):
    return jnp.minimum(x, 0.0) - jnp.log1p(jnp.exp(-jnp.abs(x)))


def _iota2(shape, dim):
    return lax.broadcasted_iota(jnp.int32, shape, dim)


def _f01(mask):
    return jnp.where(mask, 1.0, 0.0).astype(F32)


def _shift_rows(x, prev_row, next_row):
    c = x.shape[0]
    r = _iota2(x.shape, 0)
    xp = jnp.where(r == 0, prev_row, pltpu.roll(x, 1, 0))
    xn = jnp.where(r == c - 1, next_row, pltpu.roll(x, c - 1, 0))
    return xp, xn


def _conv3_chunk(ref, w_ref, start, c, t_total, bias=None, lanes=slice(None)):
    x = ref[pl.ds(start, c), lanes]
    lo = pl.multiple_of(jnp.maximum(start - 8, 0), 8)
    hi = pl.multiple_of(jnp.minimum(start + c, t_total - 8), 8)
    prev = ref[pl.ds(lo, 8), lanes][7:8] * (start > 0).astype(F32)
    nxt = ref[pl.ds(hi, 8), lanes][0:1] * (start + c < t_total).astype(F32)
    xp, xn = _shift_rows(x, prev, nxt)
    y = xp * w_ref[0:1, lanes] + x * w_ref[1:2, lanes] + xn * w_ref[2:3, lanes]
    if bias is not None:
        y = y + bias
    return y


def _cparams(sem):
    return pltpu.CompilerParams(dimension_semantics=sem, vmem_limit_bytes=VMEM_LIMIT)


def _ada_kernel(c_ref, w_ref, b_ref, o_ref):
    a = _silu(c_ref[...])
    o_ref[0] = _mm3(a, w_ref[0]) + b_ref[0]


def _ada_mod(cvec16, ada_w, ada_b):
    tn = 512
    n3 = ada_w.shape[-1]
    return pl.pallas_call(
        _ada_kernel,
        out_shape=jax.ShapeDtypeStruct((DEPTH, 16, n3), F32),
        grid=(DEPTH, n3 // tn),
        in_specs=[pl.BlockSpec((16, D_MODEL), lambda l, j: (0, 0)),
                  pl.BlockSpec((1, D_MODEL, tn), lambda l, j: (l, 0, j)),
                  pl.BlockSpec((1, 1, tn), lambda l, j: (l, 0, j))],
        out_specs=pl.BlockSpec((1, 16, tn), lambda l, j: (l, 0, j)),
        compiler_params=_cparams(("arbitrary", "arbitrary")),
        name="ada_mod",
    )(cvec16, ada_w, ada_b.reshape(DEPTH, 1, n3))


def _norm_mod_kernel(x_ref, g_ref, m_ref, o_ref):
    x = x_ref[0]
    y = x * lax.rsqrt(jnp.mean(x * x, axis=-1, keepdims=True) + 1e-6)
    y = y * g_ref[...]
    o_ref[...] = (y * (1.0 + m_ref[0, 1:2, :]) + m_ref[0, 0:1, :]).astype(o_ref.dtype)


def _norm_mod(x, g, mod, shared):
    b, t, d = x.shape
    tt = 256
    nt = t // tt
    mmap = (lambda i, j: (0, 0, 0)) if shared else (lambda i, j: (i, 0, 0))
    return pl.pallas_call(
        _norm_mod_kernel,
        out_shape=jax.ShapeDtypeStruct((b * t, d), MXU_DTYPE),
        grid=(b, nt),
        in_specs=[pl.BlockSpec((1, tt, d), lambda i, j: (i, j, 0)),
                  pl.BlockSpec((1, d), lambda i, j: (0, 0)),
                  pl.BlockSpec((1, 3, d), mmap)],
        out_specs=pl.BlockSpec((tt, d), lambda i, j: (i * nt + j, 0)),
        compiler_params=_cparams(("arbitrary", "arbitrary")),
        name="norm_mod",
    )(x, g.reshape(1, d), mod)


def _proj_kernel(h_ref, w_ref, o_ref):
    o_ref[...] = jnp.dot(h_ref[...], w_ref[...], preferred_element_type=F32)


def _proj(h, w):
    m, d = h.shape
    n = w.shape[1]
    tm = 1024
    return pl.pallas_call(
        _proj_kernel,
        out_shape=jax.ShapeDtypeStruct((m, n), F32),
        grid=(m // tm, n // TN_PROJ),
        in_specs=[pl.BlockSpec((tm, d), lambda i, j: (i, 0)),
                  pl.BlockSpec((d, TN_PROJ), lambda i, j: (0, j))],
        out_specs=pl.BlockSpec((tm, TN_PROJ), lambda i, j: (i, j)),
        compiler_params=_cparams(("arbitrary", "arbitrary")),
        name="in_proj",
    )(h, w)


def _out_kernel(y1_ref, y2_ref, w1_ref, w2_ref, x_ref, m_ref, o_ref):
    acc = jnp.dot(y1_ref[...], w1_ref[...], preferred_element_type=F32)
    acc = acc + jnp.dot(y2_ref[...], w2_ref[...], preferred_element_type=F32)
    o_ref[...] = x_ref[...] + m_ref[0, 2:3, :] * acc


def _out_proj(y1, y2, w1, w2, x2d, mod, t, shared):
    m, d = x2d.shape
    k1, k2 = y1.shape[1], y2.shape[1]
    tm, tn = 1024, 512
    per_b = t // tm if t >= tm else 1
    mmap = (lambda i, j: (0, 0, j)) if shared else (lambda i, j: (i // per_b, 0, j))
    return pl.pallas_call(
        _out_kernel,
        out_shape=jax.ShapeDtypeStruct((m, d), F32),
        grid=(m // tm, d // tn),
        in_specs=[pl.BlockSpec((tm, k1), lambda i, j: (i, 0)),
                  pl.BlockSpec((tm, k2), lambda i, j: (i, 0)),
                  pl.BlockSpec((k1, tn), lambda i, j: (0, j)),
                  pl.BlockSpec((k2, tn), lambda i, j: (0, j)),
                  pl.BlockSpec((tm, tn), lambda i, j: (i, j)),
                  pl.BlockSpec((1, 3, tn), mmap)],
        out_specs=pl.BlockSpec((tm, tn), lambda i, j: (i, j)),
        compiler_params=_cparams(("arbitrary", "arbitrary")),
        name="out_proj",
    )(y1, y2, w1, w2, x2d, mod)


def _final_norm_kernel(x_ref, g_ref, o_ref):
    x = x_ref[...]
    o_ref[...] = x * lax.rsqrt(jnp.mean(x * x, axis=-1, keepdims=True) + 1e-6) * g_ref[...]


def _final_norm(x2d, g):
    m, d = x2d.shape
    tt = 256
    return pl.pallas_call(
        _final_norm_kernel,
        out_shape=jax.ShapeDtypeStruct((m, d), F32),
        grid=(m // tt,),
        in_specs=[pl.BlockSpec((tt, d), lambda i: (i, 0)),
                  pl.BlockSpec((1, d), lambda i: (0, 0))],
        out_specs=pl.BlockSpec((tt, d), lambda i: (i, 0)),
        compiler_params=_cparams(("arbitrary",)),
        name="final_norm",
    )(x2d, g.reshape(1, d))


def _gla_consts(c, s):
    row, col = _iota2((c, c), 0), _iota2((c, c), 1)
    same = (col // s) == (row // s)
    out = {}
    for rev in (False, True):
        tri = (col >= row) if rev else (col <= row)
        out[rev] = (jnp.concatenate([_f01(tri), _f01(jnp.logical_and(tri, same))], axis=0), _f01(tri))
    return out


def _gla_chunks(chains, consts, c, s):
    nb = c // s
    rowi = _iota2((c, LANE), 0)
    cums = [_sel2_l(consts[ch["rev"]][0], ch["l"]) for ch in chains]
    score_blocks = []
    for ch, x in zip(chains, cums):
        g, cb = x[:c], x[c:]
        ch["g"] = g
        ch["gl"] = g[0:1] if ch["rev"] else g[c - 1:c]
        qt = ch["q"] * jnp.exp(cb)
        blocks = []
        for i in range(nb):
            if ch["rev"]:
                g_ref = g[(i + 1) * s:(i + 1) * s + 1] if i < nb - 1 else jnp.zeros((1, LANE), F32)
                valid = rowi >= i * s
            else:
                g_ref = g[i * s - 1:i * s] if i > 0 else jnp.zeros((1, LANE), F32)
                valid = rowi < (i + 1) * s
            kh = ch["k"] * jnp.exp(jnp.where(valid, g_ref - g, -1e30))
            blocks.append(_mm_nt(qt[i * s:(i + 1) * s], kh))
        score_blocks.append(blocks)
    inter = [_mm_nt(ch["q"] * jnp.exp(ch["g"]), ch["st"]) for ch in chains]
    upd = [_mm_tn(ch["v"], ch["k"] * jnp.exp(ch["gl"] - ch["g"])) for ch in chains]
    out = []
    for ch, blocks, oi, up in zip(chains, score_blocks, inter, upd):
        p = jnp.where(consts[ch["rev"]][1] > 0.5, jnp.concatenate(blocks, axis=0), 0.0)
        out.append((_mm(p, ch["v"]) + oi, ch["st"] * jnp.exp(ch["gl"]) + up))
    return out


def _gla_kernel(q_ref, k_ref, v_ref, z_ref, sm_ref, wg_ref, bg_ref, ng_ref, s0f_ref, s0b_ref,
                y_ref, sf_ref, sb_ref, o_scr, *, t, c, s, hb):
    n = t // c
    consts = _gla_consts(c, s)

    def body(i, carry):
        chains = []
        for j in range(hb):
            kl, vl = slice(j * K_A, (j + 1) * K_A), slice(j * V_A, (j + 1) * V_A)
            for d in range(2):
                a = pl.multiple_of((i if d == 0 else n - 1 - i) * c, c)
                logit = _mm(sm_ref[pl.ds(a, c), :], wg_ref[d, :, kl]) + bg_ref[d, :, kl]
                chains.append(dict(a=a, rev=d == 1, st=carry[2 * j + d],
                                   l=_log_sigmoid(logit) * (1.0 / GLA_GATE_NORM),
                                   q=q_ref[pl.ds(a, c), kl] * (K_A ** -0.5), k=k_ref[pl.ds(a, c), kl],
                                   v=v_ref[pl.ds(a, c), vl]))
        res = _gla_chunks(chains, consts, c, s)
        for sl, (ch, (o, _)) in enumerate(zip(chains, res)):
            o_scr[sl, pl.ds(ch["a"], c), :] = o
        return tuple(st for _, st in res)

    init = []
    for j in range(hb):
        init += [s0f_ref[0, j], s0b_ref[0, j]]
    fin = lax.fori_loop(0, n, body, tuple(init))
    for j in range(hb):
        sf_ref[0, j] = fin[2 * j]
        sb_ref[0, j] = fin[2 * j + 1]

    def epi(i, _):
        a = pl.multiple_of(i * c, c)
        for j in range(hb):
            vl = slice(j * V_A, (j + 1) * V_A)
            o = o_scr[2 * j, pl.ds(a, c), :] + o_scr[2 * j + 1, pl.ds(a, c), :]
            y = o * lax.rsqrt(jnp.mean(o * o, axis=-1, keepdims=True) + 1e-6) * ng_ref[...]
            y_ref[pl.ds(a, c), vl] = (y * _silu(z_ref[pl.ds(a, c), vl])).astype(y_ref.dtype)
        return 0

    lax.fori_loop(0, n, epi, 0)


def _gla(p, b, t, wg, bg, ng, s0f, s0b):
    c, s, hb = CHUNK_A, SUB_A, HEADS_A
    ngrp = H_A // hb
    wk, wv = hb * K_A, hb * V_A
    q_off, k_off = 0, WA_QK // wk
    v_off, z_off = (2 * WA_QK) // wv, (2 * WA_QK + WA_V) // wv
    sm_blk = AB_MAIN // LANE
    st_shape = jax.ShapeDtypeStruct((b, H_A, V_A, K_A), F32)
    st_spec = pl.BlockSpec((1, hb, V_A, K_A), lambda i, h: (i, h, 0, 0))
    return pl.pallas_call(
        functools.partial(_gla_kernel, t=t, c=c, s=s, hb=hb),
        out_shape=(jax.ShapeDtypeStruct((b * t, WA_V), MXU_DTYPE), st_shape, st_shape),
        grid=(b, ngrp),
        in_specs=[pl.BlockSpec((t, wk), lambda i, h: (i, q_off + h)),
                  pl.BlockSpec((t, wk), lambda i, h: (i, k_off + h)),
                  pl.BlockSpec((t, wv), lambda i, h: (i, v_off + h)),
                  pl.BlockSpec((t, wv), lambda i, h: (i, z_off + h)),
                  pl.BlockSpec((t, LANE), lambda i, h: (i, sm_blk)),
                  pl.BlockSpec((2, LANE, wk), lambda i, h: (0, 0, h)),
                  pl.BlockSpec((2, 1, wk), lambda i, h: (0, 0, h)),
                  pl.BlockSpec((1, V_A), lambda i, h: (0, 0)),
                  st_spec, st_spec],
        out_specs=(pl.BlockSpec((t, wv), lambda i, h: (i, h)), st_spec, st_spec),
        scratch_shapes=[pltpu.VMEM((2 * hb, t, V_A), F32)],
        compiler_params=_cparams(("arbitrary", "arbitrary")),
        name="gla_scan",
    )(p, p, p, p, p, wg, bg, ng, s0f, s0b)


def _tri_inverse_minus_eye(mats, c, masks):
    nm = [-a * masks["diag"] for a in mats]
    p2 = [_mm(n, n) for n in nm]
    y = [n + p + _mm(n, p) for n, p in zip(nm, p2)]
    p4 = [_mm(p, p) for p in p2]
    y = [yy + p + _mm(yy, p) for yy, p in zip(y, p4)]
    for coupled in masks["levels"]:
        lm = [a * coupled for a in mats]
        m = [l + _mm(l, yy) for l, yy in zip(lm, y)]
        y = [yy - (mm + _mm(yy, mm)) for yy, mm in zip(y, m)]
    return y


def _gdn_masks(c):
    row, col = _iota2((c, c), 0), _iota2((c, c), 1)
    levels, b = [], INV_BASE
    while b < c:
        levels.append(_f01(jnp.logical_and((row // (2 * b)) == (col // (2 * b)), (row // b) != (col // b))))
        b *= 2
    tri = {False: (_f01(col <= row), _f01(row > col), _f01(col <= row), _f01(col < row)),
           True: (_f01(col >= row), _f01(row < col), _f01(col >= row), _f01(col > row))}
    return dict(diag=_f01((row // INV_BASE) == (col // INV_BASE)), levels=levels, tri=tri)


def _gdn_prep(chains, c, masks):
    tri = masks["tri"]
    dg = [_sel2_l(tri[ch["rev"]][0], jnp.concatenate([ch["la"] * tri[ch["rev"]][1], ch["la"]], axis=1))
          for ch in chains]
    mats = []
    for ch, x in zip(chains, dg):
        g = x[:, c:]
        ch["g"], ch["eg"] = g, jnp.exp(g)
        ch["gl"] = g[0:1] if ch["rev"] else g[c - 1:c]
        ch["dec"] = jnp.exp(x[:, :c]) * tri[ch["rev"]][2]
        mats.append(ch["kk"] * ch["beta"] * ch["dec"] * tri[ch["rev"]][3])
    ys = _tri_inverse_minus_eye(mats, c, masks)
    rhs = [jnp.concatenate([ch["vc"] * ch["beta"], ch["kc"] * ch["beta"] * ch["eg"]], axis=1) for ch in chains]
    uw = [r + _mm(y, r) for r, y in zip(rhs, ys)]
    out = []
    for ch, x in zip(chains, uw):
        out.append(dict(u=x[:, :c], w=x[:, c:], attn=ch["qk"] * ch["dec"], qg=ch["qc"] * ch["eg"],
                        kd_t=(ch["kc"] * jnp.exp(ch["gl"] - ch["g"])).T, egl=jnp.exp(ch["gl"])))
    return out


def _l2n(x):
    return x * lax.rsqrt(jnp.sum(x * x, axis=-1, keepdims=True) + 1e-6)


def _gdn_kernel(q_ref, k_ref, v_ref, z_ref, sm_ref, wq_ref, wk_ref, wv_ref, dtb_ref, alog_ref, ng_ref,
                s0f_ref, s0b_ref, y_ref, sf_ref, sb_ref,
                u_scr, wq_scr, at_scr, kd_scr, eg_scr, o_scr, *, t, c, hb):
    n = t // c
    grp = pl.program_id(1)
    lane = _iota2((1, LANE), 1)
    is_beta = jnp.logical_and(lane >= AB_BF, lane < AB_AF)
    neg_a = -jnp.exp(alog_ref[...])
    srow, scol = _iota2((LANE, 4 * LANE), 0), _iota2((LANE, 4 * LANE), 1)
    base = jnp.where(scol < LANE, AB_BF, jnp.where(scol < 2 * LANE, AB_BB,
                                                   jnp.where(scol < 3 * LANE, AB_AF, AB_AB)))
    sels = [_f01(srow == base + (grp * hb + j)) for j in range(hb)]
    masks = _gdn_masks(c)

    def prep(i, _):
        chains = []
        for cc in range(PREP_CHUNKS_B):
            ci = i * PREP_CHUNKS_B + cc
            a = pl.multiple_of(ci * c, c)
            sm = sm_ref[pl.ds(a, c), :]
            gt = jnp.where(is_beta, _sigmoid(sm), neg_a * _softplus(sm + dtb_ref[...]))
            for j in range(hb):
                ln = slice(j * K_B, (j + 1) * K_B)
                qc = _l2n(_silu(_conv3_chunk(q_ref, wq_ref, a, c, t, lanes=ln))) * (K_B ** -0.5)
                kc = _l2n(_silu(_conv3_chunk(k_ref, wk_ref, a, c, t, lanes=ln)))
                vc = _silu(_conv3_chunk(v_ref, wv_ref, a, c, t, lanes=ln))
                kq = _mm_nt(jnp.concatenate([kc, qc], axis=0), kc)
                g4 = _sel2_r(gt, sels[j])
                for d in range(2):
                    chains.append(dict(a=a, ci=ci, s=2 * j + d, rev=d == 1, kk=kq[:c], qk=kq[c:],
                                       kc=kc, qc=qc, vc=vc, beta=g4[:, d * LANE:(d + 1) * LANE],
                                       la=g4[:, (2 + d) * LANE:(3 + d) * LANE]))
        for ch, r in zip(chains, _gdn_prep(chains, c, masks)):
            s, a = ch["s"], ch["a"]
            u_scr[s, pl.ds(a, c), :] = r["u"]
            a2 = pl.multiple_of(2 * a, 2 * c)
            wq_scr[s, pl.ds(a2, c), :] = r["w"].astype(wq_scr.dtype)
            wq_scr[s, pl.ds(a2 + c, c), :] = r["qg"].astype(wq_scr.dtype)
            at_scr[s, pl.ds(a, c), :] = r["attn"].astype(at_scr.dtype)
            kd_scr[s, :, pl.ds(a, c)] = r["kd_t"].astype(kd_scr.dtype)
            eg_scr[s, pl.ds(pl.multiple_of(ch["ci"] * 8, 8), 8), :] = jnp.broadcast_to(r["egl"], (8, LANE))
        return 0

    lax.fori_loop(0, n // PREP_CHUNKS_B, prep, 0)

    def body(i, carry):
        cis = [i if s % 2 == 0 else n - 1 - i for s in range(2 * hb)]
        starts = [pl.multiple_of(ci * c, c) for ci in cis]
        ws = [jnp.dot(wq_scr[s, pl.ds(pl.multiple_of(2 * starts[s], 2 * c), 2 * c), :], _mx(carry[s]),
                      preferred_element_type=F32) for s in range(2 * hb)]
        v_new = [_mx(u_scr[s, pl.ds(starts[s], c), :] - ws[s][:c]) for s in range(2 * hb)]
        out = []
        for s in range(2 * hb):
            o_scr[s, pl.ds(starts[s], c), :] = ws[s][c:] + jnp.dot(
                at_scr[s, pl.ds(starts[s], c), :], v_new[s], preferred_element_type=F32)
        for s in range(2 * hb):
            egl = eg_scr[s, pl.ds(pl.multiple_of(cis[s] * 8, 8), 1), :]
            out.append(carry[s] * egl + jnp.dot(kd_scr[s, :, pl.ds(starts[s], c)], v_new[s],
                                                preferred_element_type=F32))
        return tuple(out)

    init = []
    for j in range(hb):
        init += [s0f_ref[0, j], s0b_ref[0, j]]
    fin = lax.fori_loop(0, n, body, tuple(init))
    for j in range(hb):
        sf_ref[0, j] = fin[2 * j]
        sb_ref[0, j] = fin[2 * j + 1]

    def epi(i, _):
        a = pl.multiple_of(i * c, c)
        for j in range(hb):
            ln = slice(j * V_B, (j + 1) * V_B)
            o = o_scr[2 * j, pl.ds(a, c), :] + o_scr[2 * j + 1, pl.ds(a, c), :]
            y = o * lax.rsqrt(jnp.mean(o * o, axis=-1, keepdims=True) + 1e-6) * ng_ref[...]
            y_ref[pl.ds(a, c), ln] = (y * _silu(z_ref[pl.ds(a, c), ln])).astype(y_ref.dtype)
        return 0

    lax.fori_loop(0, n, epi, 0)


def _gdn(p, b, t, conv_w, dtb_row, alog_row, ng, s0f, s0b):
    c, hb = CHUNK_B, HEADS_B
    wblk = hb * K_B
    base = (2 * WA_QK + 2 * WA_V) // wblk
    ng_ = H_B // hb
    q_off, k_off, v_off, z_off = base, base + ng_, base + 2 * ng_, base + 3 * ng_
    sm_blk = AB_MAIN // LANE
    st_shape = jax.ShapeDtypeStruct((b, H_B, K_B, V_B), F32)
    st_spec = pl.BlockSpec((1, hb, K_B, V_B), lambda i, h: (i, h, 0, 0))
    row_spec = pl.BlockSpec((1, LANE), lambda i, h: (0, 0))
    ns = 2 * hb
    return pl.pallas_call(
        functools.partial(_gdn_kernel, t=t, c=c, hb=hb),
        out_shape=(jax.ShapeDtypeStruct((b * t, WB), MXU_DTYPE), st_shape, st_shape),
        grid=(b, ng_),
        in_specs=[pl.BlockSpec((t, wblk), lambda i, h: (i, q_off + h)),
                  pl.BlockSpec((t, wblk), lambda i, h: (i, k_off + h)),
                  pl.BlockSpec((t, wblk), lambda i, h: (i, v_off + h)),
                  pl.BlockSpec((t, wblk), lambda i, h: (i, z_off + h)),
                  pl.BlockSpec((t, LANE), lambda i, h: (i, sm_blk)),
                  pl.BlockSpec((3, wblk), lambda i, h: (0, h)),
                  pl.BlockSpec((3, wblk), lambda i, h: (0, ng_ + h)),
                  pl.BlockSpec((3, wblk), lambda i, h: (0, 2 * ng_ + h)),
                  row_spec, row_spec, row_spec, st_spec, st_spec],
        out_specs=(pl.BlockSpec((t, wblk), lambda i, h: (i, h)), st_spec, st_spec),
        scratch_shapes=[pltpu.VMEM((ns, t, V_B), F32),
                        pltpu.VMEM((ns, 2 * t, K_B), MXU_DTYPE),
                        pltpu.VMEM((ns, t, c), MXU_DTYPE),
                        pltpu.VMEM((ns, K_B, t), MXU_DTYPE),
                        pltpu.VMEM((ns, (t // c) * 8, LANE), F32),
                        pltpu.VMEM((ns, t, V_B), F32)],
        compiler_params=_cparams(("arbitrary", "arbitrary")),
        name="gdn_scan",
    )(p, p, p, p, p, conv_w, conv_w, conv_w, dtb_row, alog_row, ng, s0f, s0b)


def _rotary(x, cos, sin_signed):
    even = (_iota2(x.shape, 1) % 2) == 0
    swapped = jnp.where(even, pltpu.roll(x, LANE - 1, 1), pltpu.roll(x, 1, 1))
    return x * cos + swapped * sin_signed


def _ret_kernel(q_ref, k_ref, v_ref, z_ref, cos_ref, sin_ref, ng_ref, nb_ref, s0f_ref, s0b_ref,
                y_ref, sf_ref, sb_ref, of_scr, ob_scr, *, t, c, on_grid):
    n = t // c
    h = pl.program_id(1)
    hf = jnp.full((1, 1), h, jnp.int32).astype(F32)
    lg_f = jnp.log1p(-jnp.exp2(-5.0 - hf))
    lg_b = jnp.log1p(-jnp.exp2(-5.0 - (H_C - 1.0 - hf)))
    row, col = _iota2((c, c), 0), _iota2((c, c), 1)
    dmat_f = jnp.where(col <= row, jnp.exp((row - col).astype(F32) * lg_f), 0.0)
    dmat_b = jnp.where(col >= row, jnp.exp((col - row).astype(F32) * lg_b), 0.0)
    r1 = _iota2((c, 1), 0).astype(F32)
    eg_f, ed_f = jnp.exp((r1 + 1.0) * lg_f), jnp.exp((c - 1.0 - r1) * lg_f)
    eg_b, ed_b = jnp.exp((c - r1) * lg_b), jnp.exp(r1 * lg_b)
    tot_f, tot_b = jnp.exp(c * lg_f), jnp.exp(c * lg_b)

    def load_qk(start):
        q = q_ref[pl.ds(start, c), :]
        k = k_ref[pl.ds(start, c), :] * (K_C ** -0.5)
        if on_grid:
            cs, sn = cos_ref[pl.ds(start, c), :], sin_ref[pl.ds(start, c), :]
            q, k = _rotary(q, cs, sn), _rotary(k, cs, sn)
        return q, k

    def body(i, carry):
        st_f, st_b = carry
        a = pl.multiple_of(i * c, c)
        b = pl.multiple_of((n - 1 - i) * c, c)
        (qf, kf), (qb, kb) = load_qk(a), load_qk(b)
        vf, vb = v_ref[pl.ds(a, c), :], v_ref[pl.ds(b, c), :]
        sc_f, sc_b = _mm_nt(qf, kf), _mm_nt(qb, kb)
        in_f, in_b = _mm(qf * eg_f, st_f), _mm(qb * eg_b, st_b)
        up_f, up_b = _mm_tn(kf * ed_f, vf), _mm_tn(kb * ed_b, vb)
        of_scr[pl.ds(a, c), :] = _mm(sc_f * dmat_f, vf) + in_f
        ob_scr[pl.ds(b, c), :] = _mm(sc_b * dmat_b, vb) + in_b
        return st_f * tot_f + up_f, st_b * tot_b + up_b

    st_f, st_b = lax.fori_loop(0, n, body, (s0f_ref[0, 0], s0b_ref[0, 0]))
    sf_ref[0, 0] = st_f
    sb_ref[0, 0] = st_b

    def epi(i, _):
        a = pl.multiple_of(i * c, c)
        o = of_scr[pl.ds(a, c), :] + ob_scr[pl.ds(a, c), :]
        mu = jnp.mean(o, axis=-1, keepdims=True)
        d = o - mu
        var = jnp.mean(d * d, axis=-1, keepdims=True)
        y = d * lax.rsqrt(var + 1e-5) * ng_ref[...] + nb_ref[...]
        y_ref[pl.ds(a, c), :] = (y * _silu(z_ref[pl.ds(a, c), :])).astype(y_ref.dtype)
        return 0

    lax.fori_loop(0, n, epi, 0)


def _ret(p, b, t, cos_t, sin_t, ng, nb, s0f, s0b, on_grid):
    c = min(CHUNK_C, t)
    q_off, k_off = 0, WC_QK // K_C
    v_off, z_off = (2 * WC_QK) // V_C, (2 * WC_QK + WC_V) // V_C
    st_shape = jax.ShapeDtypeStruct((b, H_C, K_C, V_C), F32)
    st_spec = pl.BlockSpec((1, 1, K_C, V_C), lambda i, h: (i, h, 0, 0))
    tab_spec = pl.BlockSpec((t, K_C), lambda i, h: (0, 0))
    vrow = pl.BlockSpec((1, V_C), lambda i, h: (0, 0))
    return pl.pallas_call(
        functools.partial(_ret_kernel, t=t, c=c, on_grid=on_grid),
        out_shape=(jax.ShapeDtypeStruct((b * t, WC_V), MXU_DTYPE), st_shape, st_shape),
        grid=(b, H_C),
        in_specs=[pl.BlockSpec((t, K_C), lambda i, h: (i, q_off + h)),
                  pl.BlockSpec((t, K_C), lambda i, h: (i, k_off + h)),
                  pl.BlockSpec((t, V_C), lambda i, h: (i, v_off + h)),
                  pl.BlockSpec((t, V_C), lambda i, h: (i, z_off + h)),
                  tab_spec, tab_spec, vrow, vrow, st_spec, st_spec],
        out_specs=(pl.BlockSpec((t, V_C), lambda i, h: (i, h)), st_spec, st_spec),
        scratch_shapes=[pltpu.VMEM((t, V_C), F32), pltpu.VMEM((t, V_C), F32)],
        compiler_params=_cparams(("arbitrary", "arbitrary")),
        name="ret_scan",
    )(p, p, p, p, cos_t, sin_t, ng, nb, s0f, s0b)


def _ssd_chunks(chains, states, consts, expands, c):
    hp = HEADS_PER_GROUP
    w = hp * P_D
    gates = [_sel2_r(ch["dt_la"], expands[ch["rev"]]) for ch in chains]
    for ch, gt in zip(chains, gates):
        ch["v"], ch["la_x"] = ch["x"] * gt[:c], gt[c:]
    cums = [_sel2_l(consts[ch["rev"]][0], jnp.concatenate([ch["la_x"] * consts[ch["rev"]][1], ch["la_x"]], axis=1))
            for ch in chains]
    scores = [_mm_nt(ch["cm"], jnp.concatenate([ch["bd"]] * hp, axis=0)) for ch in chains]
    for ch, x, sc in zip(chains, cums, scores):
        gx = x[:, w:]
        ch["gx"], ch["gl"] = gx, (gx[0:1] if ch["rev"] else gx[c - 1:c])
        ch["p"] = sc * jnp.exp(x[:, :w]) * consts[ch["rev"]][2]
    intra = [_mm(ch["p"], jnp.concatenate([ch["v"]] * hp, axis=0) * consts[ch["rev"]][3]) for ch in chains]
    upd = [_mm_tn(ch["bd"], ch["v"] * jnp.exp(ch["gl"] - ch["gx"])) for ch in chains]
    outs, states = [], dict(states)
    for ch, oi, up in zip(chains, intra, upd):
        st = states[ch["rev"]]
        outs.append(oi + _mm(ch["cm"], st) * jnp.exp(ch["gx"]))
        states[ch["rev"]] = st * jnp.exp(ch["gl"]) + up
    return outs, states


def _ssd_kernel(x_ref, b_ref, c_ref, z_ref, sm_ref, wx_ref, wb_ref, wc_ref, bx_ref, bb_ref, bc_ref,
                dtb_ref, alog_ref, dsk_ref, ng_ref, s0f_ref, s0b_ref,
                y_ref, sf_ref, sb_ref, xs, bs, cs, of_scr, ob_scr, *, t, c):
    n = t // c
    hp = HEADS_PER_GROUP
    w = hp * P_D
    grp = pl.program_id(1)
    neg_a = -jnp.exp(alog_ref[...])
    erow, ecol = _iota2((LANE, w), 0), _iota2((LANE, w), 1)
    expand_f = _f01(erow == CD_DTF + grp * hp + ecol // P_D)
    expand_b = _f01(erow == CD_DTB + grp * hp + ecol // P_D)

    def prep(i, _):
        a = pl.multiple_of(i * c, c)
        xs[pl.ds(a, c), :] = _silu(_conv3_chunk(x_ref, wx_ref, a, c, t, bx_ref[...]))
        bs[pl.ds(a, c), :] = _silu(_conv3_chunk(b_ref, wb_ref, a, c, t, bb_ref[...]))
        cs[pl.ds(a, c), :] = _silu(_conv3_chunk(c_ref, wc_ref, a, c, t, bc_ref[...]))
        return 0

    lax.fori_loop(0, n, prep, 0)

    row, col = _iota2((c, c), 0), _iota2((c, c), 1)
    rt, ct = _iota2((c, hp * c), 0), _iota2((c, hp * c), 1) % c
    brow, bcol = _iota2((hp * c, w), 0), _iota2((hp * c, w), 1)
    bd_mask = _f01((brow // c) == (bcol // P_D))
    consts = {False: (_f01(col <= row), _f01(rt > ct), _f01(ct <= rt), bd_mask),
              True: (_f01(col >= row), _f01(rt < ct), _f01(ct >= rt), bd_mask)}
    expands = {False: expand_f, True: expand_b}
    o_scrs = {False: of_scr, True: ob_scr}

    def body(i, carry):
        chains = []
        for cc in range(SSD_CHUNKS):
            for rev in (False, True):
                ci = i * SSD_CHUNKS + cc
                a = pl.multiple_of((n - 1 - ci if rev else ci) * c, c)
                dt = _softplus(sm_ref[pl.ds(a, c), :] + dtb_ref[...])
                chains.append(dict(a=a, rev=rev, cm=cs[pl.ds(a, c), :], bd=bs[pl.ds(a, c), :],
                                   x=xs[pl.ds(a, c), :], dt_la=jnp.concatenate([dt, neg_a * dt], axis=0)))
        outs, states = _ssd_chunks(chains, {False: carry[0], True: carry[1]}, consts, expands, c)
        for ch, o in zip(chains, outs):
            o_scrs[ch["rev"]][pl.ds(ch["a"], c), :] = o
        return states[False], states[True]

    st_f, st_b = lax.fori_loop(0, n // SSD_CHUNKS, body, (s0f_ref[0, 0], s0b_ref[0, 0]))
    sf_ref[0, 0] = st_f
    sb_ref[0, 0] = st_b

    def epi(i, _):
        a = pl.multiple_of(i * c, c)
        o = of_scr[pl.ds(a, c), :] + ob_scr[pl.ds(a, c), :] + dsk_ref[...] * xs[pl.ds(a, c), :]
        o = o * _silu(z_ref[pl.ds(a, c), :])
        y = o * lax.rsqrt(jnp.mean(o * o, axis=-1, keepdims=True) + 1e-6) * ng_ref[...]
        y_ref[pl.ds(a, c), :] = y.astype(y_ref.dtype)
        return 0

    lax.fori_loop(0, n, epi, 0)


def _ssd(p, b, t, conv_w, conv_b, dtb_row, alog_row, dskip, ng, s0f, s0b):
    c = CHUNK_D
    w = HEADS_PER_GROUP * P_D
    base = 2 * WC_QK + 2 * WC_V
    x_off = base // w
    b_off = (base + WD) // N_D
    c_off = (base + WD + WD_BC) // N_D
    z_off = (base + WD + 2 * WD_BC) // w
    sm_blk = CD_MAIN // LANE
    st_shape = jax.ShapeDtypeStruct((b, G_D, N_D, w), F32)
    st_spec = pl.BlockSpec((1, 1, N_D, w), lambda i, g: (i, g, 0, 0))
    row_spec = pl.BlockSpec((1, LANE), lambda i, g: (0, 0))
    return pl.pallas_call(
        functools.partial(_ssd_kernel, t=t, c=c),
        out_shape=(jax.ShapeDtypeStruct((b * t, WD), MXU_DTYPE), st_shape, st_shape),
        grid=(b, G_D),
        in_specs=[pl.BlockSpec((t, w), lambda i, g: (i, x_off + g)),
                  pl.BlockSpec((t, N_D), lambda i, g: (i, b_off + g)),
                  pl.BlockSpec((t, N_D), lambda i, g: (i, c_off + g)),
                  pl.BlockSpec((t, w), lambda i, g: (i, z_off + g)),
                  pl.BlockSpec((t, LANE), lambda i, g: (i, sm_blk)),
                  pl.BlockSpec((3, w), lambda i, g: (0, g)),
                  pl.BlockSpec((3, N_D), lambda i, g: (0, WD // N_D + g)),
                  pl.BlockSpec((3, N_D), lambda i, g: (0, (WD + WD_BC) // N_D + g)),
                  pl.BlockSpec((1, w), lambda i, g: (0, g)),
                  pl.BlockSpec((1, N_D), lambda i, g: (0, WD // N_D + g)),
                  pl.BlockSpec((1, N_D), lambda i, g: (0, (WD + WD_BC) // N_D + g)),
                  row_spec, row_spec,
                  pl.BlockSpec((1, w), lambda i, g: (0, g)),
                  pl.BlockSpec((1, w), lambda i, g: (0, g)),
                  st_spec, st_spec],
        out_specs=(pl.BlockSpec((t, w), lambda i, g: (i, g)), st_spec, st_spec),
        scratch_shapes=[pltpu.VMEM((t, w), F32), pltpu.VMEM((t, N_D), F32), pltpu.VMEM((t, N_D), F32),
                        pltpu.VMEM((t, w), F32), pltpu.VMEM((t, w), F32)],
        compiler_params=_cparams(("arbitrary", "arbitrary")),
        name="ssd_scan",
    )(p, p, p, p, p, conv_w, conv_w, conv_w, conv_b, conv_b, conv_b, dtb_row, alog_row, dskip, ng, s0f, s0b)


def _split_cols(w, sizes):
    out, o = [], 0
    for s in sizes:
        out.append(w[..., o:o + s])
        o += s
    return out


def _pad_cols(w, n):
    return jnp.pad(w, [(0, 0)] * (w.ndim - 1) + [(0, n - w.shape[-1])])


def _lane_row(pieces):
    row = jnp.zeros((LANE,), F32)
    for off, vec in pieces:
        row = lax.dynamic_update_slice(row, vec.astype(F32), (off,))
    return row.reshape(1, LANE)


def _rotary_tables(t):
    n_rows = t // GRID_W
    rows = jnp.repeat(jnp.arange(n_rows), GRID_W).astype(F32)
    cols = jnp.tile(jnp.arange(GRID_W), n_rows).astype(F32)
    n_freq = K_C // 4
    inv_freq = ROPE_BASE ** (-jnp.arange(n_freq, dtype=F32) / n_freq)
    ang = jnp.concatenate([rows[:, None] * inv_freq, cols[:, None] * inv_freq], axis=-1)
    cos = jnp.repeat(jnp.cos(ang), 2, axis=-1)
    sin = jnp.repeat(jnp.sin(ang), 2, axis=-1)
    sign = jnp.tile(jnp.array([-1.0, 1.0], F32), K_C // 2)
    return cos, sin * sign


def kernel(x_prompt, x_sample, state_gla_fwd, state_gla_bwd, state_gdn_fwd, state_gdn_bwd, state_ret_fwd, state_ret_bwd, state_ssd_fwd, state_ssd_bwd, c, c_ctx, ada_w, ada_b, norm_g, final_norm_g, ab_w_in, ab_w_out, gla_gate_w2, gla_gate_b, gla_norm_g, gdn_conv_w, gdn_a_log, gdn_dt_bias, gdn_norm_g, cd_w_in, cd_w_out, ret_norm_g, ret_norm_b, ssd_conv_w, ssd_conv_b, ssd_a_log, ssd_dt_bias, ssd_d, ssd_norm_g):
    bp, tp, _ = x_prompt.shape
    bs, ts, _ = x_sample.shape
    hp = HEADS_PER_GROUP

    cvec = jnp.concatenate([c_ctx[None, :], c], axis=0)
    cvec16 = jnp.pad(cvec, ((0, 16 - cvec.shape[0]), (0, 0)))
    mod = _ada_mod(cvec16, ada_w, ada_b).reshape(DEPTH, 16, 3, D_MODEL)

    groups = [
        dict(x=x_prompt, b=bp, t=tp, shared=True, on_grid=False, rows=slice(0, 1)),
        dict(x=x_sample, b=bs, t=ts, shared=False, on_grid=True, rows=slice(1, 1 + bs)),
    ]
    cos_t, sin_t = _rotary_tables(ts)
    xs = [g["x"].reshape(g["b"] * g["t"], D_MODEL) for g in groups]
    new_states = {}

    for l in range(DEPTH):
        i = l // 2
        if l % 2 == 0:
            qa, ka, va, za, ra_f, ra_b, qkv_b, zb, be_f, be_b, a_f, a_b = _split_cols(ab_w_in[i], AB_SIZES)
            w_in = _pad_cols(jnp.concatenate(
                [qa, ka, va, za, qkv_b, zb, ra_f, ra_b, be_f, be_b, a_f, a_b], axis=1), AB_COLS).astype(MXU_DTYPE)
            w_o1 = ab_w_out[i][:WA_V].astype(MXU_DTYPE)
            w_o2 = ab_w_out[i][WA_V:].astype(MXU_DTYPE)
            wg = jnp.zeros((2, LANE, WA_QK), F32)
            wg = wg.at[0, AB_RF:AB_RF + GLA_RANK].set(gla_gate_w2[i, 0])
            wg = wg.at[1, AB_RB:AB_RB + GLA_RANK].set(gla_gate_w2[i, 1])
            bg = gla_gate_b[i].reshape(2, 1, WA_QK)
            dtb_row = _lane_row([(AB_AF, gdn_dt_bias[i, 0]), (AB_AB, gdn_dt_bias[i, 1])])
            alog_row = _lane_row([(AB_AF, gdn_a_log[i, 0]), (AB_AB, gdn_a_log[i, 1])])
        else:
            w_in = _pad_cols(cd_w_in[i], CD_COLS).astype(MXU_DTYPE)
            w_o1 = cd_w_out[i][:WC_V].astype(MXU_DTYPE)
            w_o2 = cd_w_out[i][WC_V:].astype(MXU_DTYPE)
            dtb_row = _lane_row([(CD_DTF, ssd_dt_bias[i, 0]), (CD_DTB, ssd_dt_bias[i, 1])])
            alog_row = _lane_row([(CD_DTF, ssd_a_log[i, 0]), (CD_DTB, ssd_a_log[i, 1])])
            dskip = jnp.repeat(ssd_d[i], P_D).reshape(1, WD)

        for gi, g in enumerate(groups):
            b, t = g["b"], g["t"]
            mod_g = mod[l, g["rows"]]
            h = _norm_mod(xs[gi].reshape(b, t, D_MODEL), norm_g[l], mod_g, g["shared"])
            p = _proj(h, w_in)
            if l % 2 == 0:
                if gi == 0:
                    sa_f = sa_b = jnp.zeros((b, H_A, V_A, K_A), F32)
                    sb_f = sb_b = jnp.zeros((b, H_B, K_B, V_B), F32)
                else:
                    sa_f = jnp.swapaxes(state_gla_fwd[:, i], -1, -2)
                    sa_b = jnp.swapaxes(state_gla_bwd[:, i], -1, -2)
                    sb_f, sb_b = state_gdn_fwd[:, i], state_gdn_bwd[:, i]
                y1, fa_f, fa_b = _gla(p, b, t, wg, bg, gla_norm_g[i].reshape(1, V_A), sa_f, sa_b)
                y2, fb_f, fb_b = _gdn(p, b, t, gdn_conv_w[i], dtb_row, alog_row,
                                      gdn_norm_g[i].reshape(1, V_B), sb_f, sb_b)
                if gi == 0:
                    new_states["gla_f"] = jnp.swapaxes(fa_f, -1, -2)
                    new_states["gla_b"] = jnp.swapaxes(fa_b, -1, -2)
                    new_states["gdn_f"], new_states["gdn_b"] = fb_f, fb_b
            else:
                if gi == 0:
                    sc_f = sc_b = jnp.zeros((b, H_C, K_C, V_C), F32)
                    sd_f = sd_b = jnp.zeros((b, G_D, N_D, hp * P_D), F32)
                else:
                    sc_f, sc_b = state_ret_fwd[:, i], state_ret_bwd[:, i]

                    def to_grp(s):
                        s = s.reshape(b, G_D, hp, N_D, P_D)
                        return jnp.transpose(s, (0, 1, 3, 2, 4)).reshape(b, G_D, N_D, hp * P_D)

                    sd_f, sd_b = to_grp(state_ssd_fwd[:, i]), to_grp(state_ssd_bwd[:, i])
                tab_c, tab_s = (cos_t, sin_t) if g["on_grid"] else (cos_t[:t], sin_t[:t])
                y1, fc_f, fc_b = _ret(p, b, t, tab_c, tab_s, ret_norm_g[i].reshape(1, V_C),
                                      ret_norm_b[i].reshape(1, V_C), sc_f, sc_b, g["on_grid"])
                y2, fd_f, fd_b = _ssd(p, b, t, ssd_conv_w[i], ssd_conv_b[i].reshape(1, -1), dtb_row, alog_row,
                                      dskip, ssd_norm_g[i].reshape(1, WD), sd_f, sd_b)
                if gi == 0:
                    def from_grp(s):
                        s = s.reshape(b, G_D, N_D, hp, P_D)
                        return jnp.transpose(s, (0, 1, 3, 2, 4)).reshape(b, H_D, N_D, P_D)

                    new_states["ret_f"], new_states["ret_b"] = fc_f, fc_b
                    new_states["ssd_f"], new_states["ssd_b"] = from_grp(fd_f), from_grp(fd_b)
            xs[gi] = _out_proj(y1, y2, w_o1, w_o2, xs[gi], mod_g, t, g["shared"])

    y_prompt = _final_norm(xs[0], final_norm_g).reshape(bp, tp, D_MODEL)
    y_sample = _final_norm(xs[1], final_norm_g).reshape(bs, ts, D_MODEL)
    st = lambda k: new_states[k][:, None]
    return (y_prompt, y_sample, st("gla_f"), st("gla_b"), st("gdn_f"), st("gdn_b"),
            st("ret_f"), st("ret_b"), st("ssd_f"), st("ssd_b"))
```

```python
import functools
import math

import jax
import jax.numpy as jnp
from jax import lax
from jax.experimental import pallas as pl
from jax.experimental.pallas import tpu as pltpu

F32 = jnp.float32
MXU_DTYPE = jnp.bfloat16

D_MODEL = 4096
DEPTH = 2
GRID_W = 64
ROPE_BASE = 10000.0
H_A, K_A, V_A = 8, 128, 256
GLA_RANK = 16
GLA_GATE_NORM = 16.0
H_B, K_B, V_B = 16, 128, 128
H_C, K_C, V_C = 8, 128, 256
H_D, N_D, P_D, G_D = 32, 128, 64, 4
HEADS_PER_GROUP = H_D // G_D
WA_QK, WA_V = H_A * K_A, H_A * V_A
WB = H_B * K_B
WC_QK, WC_V = H_C * K_C, H_C * V_C
WD, WD_BC = H_D * P_D, G_D * N_D
AB_SIZES = (WA_QK, WA_QK, WA_V, WA_V, GLA_RANK, GLA_RANK, 3 * WB, WB, H_B, H_B, H_B, H_B)
CD_SIZES = (WC_QK, WC_QK, WC_V, WC_V, WD + 2 * WD_BC, WD, H_D, H_D)

LANE = 128
VMEM_LIMIT = 52 * 1024 * 1024

AB_MAIN = 2 * WA_QK + 2 * WA_V + 4 * WB
CD_MAIN = 2 * WC_QK + 2 * WC_V + (WD + 2 * WD_BC) + WD
TN_PROJ = 1024
AB_RF, AB_RB, AB_BF, AB_BB, AB_AF, AB_AB = 0, 16, 32, 48, 64, 80
CD_DTF, CD_DTB, CD_LA = 0, 32, 64

CHUNK_A = 64
SUB_A = 16
HEADS_A = 2
GLA_CHUNKS = 4
CHUNK_B = 128
HEADS_B = 2
PREP_CHUNKS_B = 4
INV_BASE = 8
CHUNK_C = 256
CHUNK_D = P_D
SSD_CHUNKS = 4


def _mx(x):
    return x.astype(MXU_DTYPE)


def _mm(a, b):
    return jnp.dot(_mx(a), _mx(b), preferred_element_type=F32)


def _mm_nt(a, b):
    return lax.dot_general(_mx(a), _mx(b), (((1,), (1,)), ((), ())), preferred_element_type=F32)


def _mm_tn(a, b):
    return lax.dot_general(_mx(a), _mx(b), (((0,), (0,)), ((), ())), preferred_element_type=F32)


def _split3(x):
    hi = _mx(x)
    r1 = x - hi.astype(F32)
    mid = _mx(r1)
    lo = _mx(r1 - mid.astype(F32))
    return hi, mid, lo


def _sel_l(m01, x):
    hi, mid, lo = _split3(x)
    m = _mx(m01)
    d = functools.partial(jnp.dot, preferred_element_type=F32)
    return d(m, hi) + d(m, mid) + d(m, lo)


def _sel_r(x, m01):
    hi, mid, lo = _split3(x)
    m = _mx(m01)
    d = functools.partial(jnp.dot, preferred_element_type=F32)
    return d(hi, m) + d(mid, m) + d(lo, m)


def _sel2_l(m01, x):
    hi = _mx(x)
    lo = _mx(x - hi.astype(F32))
    m = _mx(m01)
    return jnp.dot(m, hi, preferred_element_type=F32) + jnp.dot(m, lo, preferred_element_type=F32)


def _sel2_r(x, m01):
    hi = _mx(x)
    lo = _mx(x - hi.astype(F32))
    m = _mx(m01)
    return jnp.dot(hi, m, preferred_element_type=F32) + jnp.dot(lo, m, preferred_element_type=F32)


def _mm3(a, b):
    a_hi = _mx(a)
    a_lo = _mx(a - a_hi.astype(F32))
    b_hi = _mx(b)
    b_lo = _mx(b - b_hi.astype(F32))
    d = functools.partial(jnp.dot, preferred_element_type=F32)
    return d(a_hi, b_hi) + d(a_hi, b_lo) + d(a_lo, b_hi)


def _silu(x):
    return x * (1.0 / (1.0 + jnp.exp(-x)))


def _sigmoid(x):
    return 1.0 / (1.0 + jnp.exp(-x))


def _softplus(x):
    return jnp.maximum(x, 0.0) + jnp.log1p(jnp.exp(-jnp.abs(x)))


def _log_sigmoid(x):
    return jnp.minimum(x, 0.0) - jnp.log1p(jnp.exp(-jnp.abs(x)))


def _iota2(shape, dim):
    return lax.broadcasted_iota(jnp.int32, shape, dim)


def _f01(mask):
    return jnp.where(mask, 1.0, 0.0).astype(F32)


def _shift_rows(x, prev_row, next_row):
    c = x.shape[0]
    r = _iota2(x.shape, 0)
    xp = jnp.where(r == 0, prev_row, pltpu.roll(x, 1, 0))
    xn = jnp.where(r == c - 1, next_row, pltpu.roll(x, c - 1, 0))
    return xp, xn


def _conv3_chunk(ref, w_ref, start, c, t_total, bias=None, lanes=slice(None)):
    x = ref[pl.ds(start, c), lanes]
    lo = pl.multiple_of(jnp.maximum(start - 8, 0), 8)
    hi = pl.multiple_of(jnp.minimum(start + c, t_total - 8), 8)
    prev = ref[pl.ds(lo, 8), lanes][7:8] * (start > 0).astype(F32)
    nxt = ref[pl.ds(hi, 8), lanes][0:1] * (start + c < t_total).astype(F32)
    xp, xn = _shift_rows(x, prev, nxt)
    y = xp * w_ref[0:1, lanes] + x * w_ref[1:2, lanes] + xn * w_ref[2:3, lanes]
    if bias is not None:
        y = y + bias
    return y


def _cparams(sem):
    return pltpu.CompilerParams(dimension_semantics=sem, vmem_limit_bytes=VMEM_LIMIT)


def _ada_kernel(c_ref, w_ref, b_ref, o_ref):
    a = _silu(c_ref[...])
    o_ref[0] = _mm3(a, w_ref[0]) + b_ref[0]


def _ada_mod(cvec16, ada_w, ada_b):
    tn = 512
    n3 = ada_w.shape[-1]
    return pl.pallas_call(
        _ada_kernel,
        out_shape=jax.ShapeDtypeStruct((DEPTH, 16, n3), F32),
        grid=(DEPTH, n3 // tn),
        in_specs=[pl.BlockSpec((16, D_MODEL), lambda l, j: (0, 0)),
                  pl.BlockSpec((1, D_MODEL, tn), lambda l, j: (l, 0, j)),
                  pl.BlockSpec((1, 1, tn), lambda l, j: (l, 0, j))],
        out_specs=pl.BlockSpec((1, 16, tn), lambda l, j: (l, 0, j)),
        compiler_params=_cparams(("arbitrary", "arbitrary")),
        name="ada_mod",
    )(cvec16, ada_w, ada_b.reshape(DEPTH, 1, n3))


def _norm_mod_kernel(x_ref, g_ref, m_ref, o_ref):
    x = x_ref[0]
    y = x * lax.rsqrt(jnp.mean(x * x, axis=-1, keepdims=True) + 1e-6)
    y = y * g_ref[...]
    o_ref[...] = (y * (1.0 + m_ref[0, 1:2, :]) + m_ref[0, 0:1, :]).astype(o_ref.dtype)


def _norm_mod(x, g, mod, shared):
    b, t, d = x.shape
    tt = 256
    nt = t // tt
    mmap = (lambda i, j: (0, 0, 0)) if shared else (lambda i, j: (i, 0, 0))
    return pl.pallas_call(
        _norm_mod_kernel,
        out_shape=jax.ShapeDtypeStruct((b * t, d), MXU_DTYPE),
        grid=(b, nt),
        in_specs=[pl.BlockSpec((1, tt, d), lambda i, j: (i, j, 0)),
                  pl.BlockSpec((1, d), lambda i, j: (0, 0)),
                  pl.BlockSpec((1, 3, d), mmap)],
        out_specs=pl.BlockSpec((tt, d), lambda i, j: (i * nt + j, 0)),
        compiler_params=_cparams(("arbitrary", "arbitrary")),
        name="norm_mod",
    )(x, g.reshape(1, d), mod)


def _gate_lanes(x, dtb, alog, kind):
    lane = _iota2((1, LANE), 1)
    neg_a = -jnp.exp(alog)
    if kind == "ab":
        return jnp.where(lane < AB_BF, x, jnp.where(lane < AB_AF, _sigmoid(x), neg_a * _softplus(x + dtb)))
    dt = _softplus(x + dtb)
    return jnp.where(lane < CD_LA, dt, pltpu.roll(neg_a * dt, CD_LA, 1))


def _proj_kernel(h_ref, w_ref, ws_ref, dtb_ref, alog_ref, o_ref, os_ref, *, kind):
    o_ref[...] = jnp.dot(h_ref[...], w_ref[...], preferred_element_type=F32)

    @pl.when(pl.program_id(1) == 0)
    def _():
        x = jnp.dot(h_ref[...], ws_ref[...], preferred_element_type=F32)
        os_ref[...] = _gate_lanes(x, dtb_ref[...], alog_ref[...], kind)


def _proj(h, w, w_small, dtb_row, alog_row, kind):
    m, d = h.shape
    n = w.shape[1]
    tm = min(1024, m)
    row_spec = pl.BlockSpec((1, LANE), lambda i, j: (0, 0))
    return pl.pallas_call(
        functools.partial(_proj_kernel, kind=kind),
        out_shape=(jax.ShapeDtypeStruct((m, n), F32), jax.ShapeDtypeStruct((m, LANE), F32)),
        grid=(m // tm, n // TN_PROJ),
        in_specs=[pl.BlockSpec((tm, d), lambda i, j: (i, 0)),
                  pl.BlockSpec((d, TN_PROJ), lambda i, j: (0, j)),
                  pl.BlockSpec((d, LANE), lambda i, j: (0, 0)),
                  row_spec, row_spec],
        out_specs=(pl.BlockSpec((tm, TN_PROJ), lambda i, j: (i, j)),
                   pl.BlockSpec((tm, LANE), lambda i, j: (i, 0))),
        compiler_params=_cparams(("arbitrary", "arbitrary")),
        name="in_proj",
    )(h, w, w_small, dtb_row, alog_row)


def _out_kernel(y1_ref, y2_ref, w1_ref, w2_ref, x_ref, m_ref, o_ref):
    acc = jnp.dot(y1_ref[...], w1_ref[...], preferred_element_type=F32)
    acc = acc + jnp.dot(y2_ref[...], w2_ref[...], preferred_element_type=F32)
    o_ref[...] = x_ref[...] + m_ref[0, 2:3, :] * acc


def _out_proj(y1, y2, w, x2d, mod, t, shared):
    m, d = x2d.shape
    k1, k2 = y1.shape[1], y2.shape[1]
    assert k1 == k2 and w.shape[0] == k1 + k2
    tm, tn = 1024, 512
    per_b = t // tm if t >= tm else 1
    mmap = (lambda i, j: (0, 0, j)) if shared else (lambda i, j: (i // per_b, 0, j))
    return pl.pallas_call(
        _out_kernel,
        out_shape=jax.ShapeDtypeStruct((m, d), F32),
        grid=(m // tm, d // tn),
        in_specs=[pl.BlockSpec((tm, k1), lambda i, j: (i, 0)),
                  pl.BlockSpec((tm, k2), lambda i, j: (i, 0)),
                  pl.BlockSpec((k1, tn), lambda i, j: (0, j)),
                  pl.BlockSpec((k2, tn), lambda i, j: (1, j)),
                  pl.BlockSpec((tm, tn), lambda i, j: (i, j)),
                  pl.BlockSpec((1, 3, tn), mmap)],
        out_specs=pl.BlockSpec((tm, tn), lambda i, j: (i, j)),
        compiler_params=_cparams(("arbitrary", "arbitrary")),
        name="out_proj",
    )(y1, y2, w, w, x2d, mod)


def _final_norm_kernel(x_ref, g_ref, o_ref):
    x = x_ref[...]
    o_ref[...] = x * lax.rsqrt(jnp.mean(x * x, axis=-1, keepdims=True) + 1e-6) * g_ref[...]


def _final_norm(x2d, g):
    m, d = x2d.shape
    tt = 256
    return pl.pallas_call(
        _final_norm_kernel,
        out_shape=jax.ShapeDtypeStruct((m, d), F32),
        grid=(m // tt,),
        in_specs=[pl.BlockSpec((tt, d), lambda i: (i, 0)),
                  pl.BlockSpec((1, d), lambda i: (0, 0))],
        out_specs=pl.BlockSpec((tt, d), lambda i: (i, 0)),
        compiler_params=_cparams(("arbitrary",)),
        name="final_norm",
    )(x2d, g.reshape(1, d))


def _gla_consts(c, s):
    row, col = _iota2((c, c), 0), _iota2((c, c), 1)
    same = (col // s) == (row // s)
    out = {}
    for rev in (False, True):
        tri = (col >= row) if rev else (col <= row)
        out[rev] = (jnp.concatenate([_f01(tri), _f01(jnp.logical_and(tri, same))], axis=0), _f01(tri))
    return out


def _gla_chunks(chains, states, consts, c, s):
    nb = c // s
    rowi = _iota2((c, LANE), 0)
    cums = [_sel2_l(consts[ch["rev"]][0], ch["l"]) for ch in chains]
    score_blocks = []
    for ch, x in zip(chains, cums):
        g, cb = x[:c], x[c:]
        ch["g"] = g
        ch["gl"] = g[0:1] if ch["rev"] else g[c - 1:c]
        qt = ch["q"] * jnp.exp(cb)
        blocks = []
        for i in range(nb):
            if ch["rev"]:
                g_ref = g[(i + 1) * s:(i + 1) * s + 1] if i < nb - 1 else jnp.zeros((1, LANE), F32)
                valid = rowi >= i * s
            else:
                g_ref = g[i * s - 1:i * s] if i > 0 else jnp.zeros((1, LANE), F32)
                valid = rowi < (i + 1) * s
            kh = ch["k"] * jnp.exp(jnp.where(valid, g_ref - g, -1e30))
            blocks.append(_mm_nt(qt[i * s:(i + 1) * s], kh))
        score_blocks.append(blocks)
    upd = [_mm_tn(ch["v"], ch["k"] * jnp.exp(ch["gl"] - ch["g"])) for ch in chains]
    intra = [_mm(jnp.where(consts[ch["rev"]][1] > 0.5, jnp.concatenate(blocks, axis=0), 0.0), ch["v"])
             for ch, blocks in zip(chains, score_blocks)]
    outs, states = [], list(states)
    for ch, oi, up in zip(chains, intra, upd):
        st = states[ch["slot"]]
        outs.append(oi + _mm_nt(ch["q"] * jnp.exp(ch["g"]), st))
        states[ch["slot"]] = st * jnp.exp(ch["gl"]) + up
    return outs, states


def _gla_kernel(q_ref, k_ref, v_ref, z_ref, sm_ref, wg_ref, bg_ref, ng_ref, s0f_ref, s0b_ref,
                y_ref, sf_ref, sb_ref, o_scr, *, t, c, s, hb):
    n = t // c
    consts = _gla_consts(c, s)

    def body(i, carry):
        chains = []
        for cc in range(GLA_CHUNKS):
            ci = i * GLA_CHUNKS + cc
            for j in range(hb):
                kl, vl = slice(j * K_A, (j + 1) * K_A), slice(j * V_A, (j + 1) * V_A)
                for d in range(2):
                    a = pl.multiple_of((ci if d == 0 else n - 1 - ci) * c, c)
                    logit = _mm(sm_ref[pl.ds(a, c), :], wg_ref[d, :, kl]) + bg_ref[d, :, kl]
                    chains.append(dict(a=a, rev=d == 1, slot=2 * j + d,
                                       l=_log_sigmoid(logit) * (1.0 / GLA_GATE_NORM),
                                       q=q_ref[pl.ds(a, c), kl] * (K_A ** -0.5), k=k_ref[pl.ds(a, c), kl],
                                       v=v_ref[pl.ds(a, c), vl]))
        outs, states = _gla_chunks(chains, carry, consts, c, s)
        for ch, o in zip(chains, outs):
            o_scr[ch["slot"], pl.ds(ch["a"], c), :] = o
        return tuple(states)

    init = []
    for j in range(hb):
        init += [s0f_ref[0, j], s0b_ref[0, j]]
    fin = lax.fori_loop(0, n // GLA_CHUNKS, body, tuple(init))
    for j in range(hb):
        sf_ref[0, j] = fin[2 * j]
        sb_ref[0, j] = fin[2 * j + 1]

    def epi(i, _):
        a = pl.multiple_of(i * c, c)
        for j in range(hb):
            vl = slice(j * V_A, (j + 1) * V_A)
            o = o_scr[2 * j, pl.ds(a, c), :] + o_scr[2 * j + 1, pl.ds(a, c), :]
            y = o * lax.rsqrt(jnp.mean(o * o, axis=-1, keepdims=True) + 1e-6) * ng_ref[...]
            y_ref[pl.ds(a, c), vl] = (y * _silu(z_ref[pl.ds(a, c), vl])).astype(y_ref.dtype)
        return 0

    lax.fori_loop(0, n, epi, 0)


def _gla(p, ps, b, t, wg, bg, ng, s0f, s0b):
    c, s, hb = CHUNK_A, SUB_A, HEADS_A
    ngrp = H_A // hb
    wk, wv = hb * K_A, hb * V_A
    q_off, k_off = 0, WA_QK // wk
    v_off, z_off = (2 * WA_QK) // wv, (2 * WA_QK + WA_V) // wv
    sm_blk = AB_MAIN // LANE
    st_shape = jax.ShapeDtypeStruct((b, H_A, V_A, K_A), F32)
    st_spec = pl.BlockSpec((1, hb, V_A, K_A), lambda i, h: (i, h, 0, 0))
    return pl.pallas_call(
        functools.partial(_gla_kernel, t=t, c=c, s=s, hb=hb),
        out_shape=(jax.ShapeDtypeStruct((b * t, WA_V), MXU_DTYPE), st_shape, st_shape),
        grid=(b, ngrp),
        in_specs=[pl.BlockSpec((t, wk), lambda i, h: (i, q_off + h)),
                  pl.BlockSpec((t, wk), lambda i, h: (i, k_off + h)),
                  pl.BlockSpec((t, wv), lambda i, h: (i, v_off + h)),
                  pl.BlockSpec((t, wv), lambda i, h: (i, z_off + h)),
                  pl.BlockSpec((t, LANE), lambda i, h: (i, 0)),
                  pl.BlockSpec((2, LANE, wk), lambda i, h: (0, 0, h)),
                  pl.BlockSpec((2, 1, wk), lambda i, h: (0, 0, h)),
                  pl.BlockSpec((1, V_A), lambda i, h: (0, 0)),
                  st_spec, st_spec],
        out_specs=(pl.BlockSpec((t, wv), lambda i, h: (i, h)), st_spec, st_spec),
        scratch_shapes=[pltpu.VMEM((2 * hb, t, V_A), F32)],
        compiler_params=_cparams(("arbitrary", "arbitrary")),
        name="gla_scan",
    )(p, p, p, p, ps, wg, bg, ng, s0f, s0b)


def _tri_inverse_minus_eye(mats, c, masks):
    nm = [-a * masks["diag"] for a in mats]
    p2 = [_mm(n, n) for n in nm]
    y = [n + p + _mm(n, p) for n, p in zip(nm, p2)]
    p4 = [_mm(p, p) for p in p2]
    y = [yy + p + _mm(yy, p) for yy, p in zip(y, p4)]
    for coupled in masks["levels"]:
        lm = [a * coupled for a in mats]
        m = [l + _mm(l, yy) for l, yy in zip(lm, y)]
        y = [yy - (mm + _mm(yy, mm)) for yy, mm in zip(y, m)]
    return y


def _gdn_masks(c):
    row, col = _iota2((c, c), 0), _iota2((c, c), 1)
    levels, b = [], INV_BASE
    while b < c:
        levels.append(_f01(jnp.logical_and((row // (2 * b)) == (col // (2 * b)), (row // b) != (col // b))))
        b *= 2
    tri = {False: (_f01(col <= row), _f01(row > col), _f01(col <= row), _f01(col < row)),
           True: (_f01(col >= row), _f01(row < col), _f01(col >= row), _f01(col > row))}
    return dict(diag=_f01((row // INV_BASE) == (col // INV_BASE)), levels=levels, tri=tri)


def _gdn_prep(chains, c, masks):
    tri = masks["tri"]
    dg = [_sel2_l(tri[ch["rev"]][0], jnp.concatenate([ch["la"] * tri[ch["rev"]][1], ch["la"]], axis=1))
          for ch in chains]
    mats = []
    for ch, x in zip(chains, dg):
        g = x[:, c:]
        ch["g"], ch["eg"] = g, jnp.exp(g)
        ch["gl"] = g[0:1] if ch["rev"] else g[c - 1:c]
        ch["dec"] = jnp.exp(x[:, :c]) * tri[ch["rev"]][2]
        mats.append(ch["kk"] * ch["beta"] * ch["dec"] * tri[ch["rev"]][3])
    ys = _tri_inverse_minus_eye(mats, c, masks)
    rhs = [jnp.concatenate([ch["vc"] * ch["beta"], ch["kc"] * ch["beta"] * ch["eg"]], axis=1) for ch in chains]
    uw = [r + _mm(y, r) for r, y in zip(rhs, ys)]
    out = []
    for ch, x in zip(chains, uw):
        out.append(dict(u=x[:, :c], w=x[:, c:], attn=ch["qk"] * ch["dec"], qg=ch["qc"] * ch["eg"],
                        kd_t=(ch["kc"] * jnp.exp(ch["gl"] - ch["g"])).T, egl=jnp.exp(ch["gl"])))
    return out


def _l2n(x):
    return x * lax.rsqrt(jnp.sum(x * x, axis=-1, keepdims=True) + 1e-6)


def _gdn_kernel(q_ref, k_ref, v_ref, z_ref, sm_ref, wq_ref, wk_ref, wv_ref, ng_ref,
                s0f_ref, s0b_ref, y_ref, sf_ref, sb_ref,
                u_scr, wq_scr, at_scr, kd_scr, eg_scr, o_scr, *, t, c, hb):
    n = t // c
    grp = pl.program_id(1)
    srow, scol = _iota2((LANE, 4 * LANE), 0), _iota2((LANE, 4 * LANE), 1)
    base = jnp.where(scol < LANE, AB_BF, jnp.where(scol < 2 * LANE, AB_BB,
                                                   jnp.where(scol < 3 * LANE, AB_AF, AB_AB)))
    sels = [_f01(srow == base + (grp * hb + j)) for j in range(hb)]
    masks = _gdn_masks(c)

    pcs = min(PREP_CHUNKS_B, n)

    def prep(i, _):
        chains = []
        for cc in range(pcs):
            ci = i * pcs + cc
            a = pl.multiple_of(ci * c, c)
            gt = sm_ref[pl.ds(a, c), :]
            for j in range(hb):
                ln = slice(j * K_B, (j + 1) * K_B)
                qc = _l2n(_silu(_conv3_chunk(q_ref, wq_ref, a, c, t, lanes=ln))) * (K_B ** -0.5)
                kc = _l2n(_silu(_conv3_chunk(k_ref, wk_ref, a, c, t, lanes=ln)))
                vc = _silu(_conv3_chunk(v_ref, wv_ref, a, c, t, lanes=ln))
                kq = _mm_nt(jnp.concatenate([kc, qc], axis=0), kc)
                g4 = _sel2_r(gt, sels[j])
                for d in range(2):
                    chains.append(dict(a=a, ci=ci, s=2 * j + d, rev=d == 1, kk=kq[:c], qk=kq[c:],
                                       kc=kc, qc=qc, vc=vc, beta=g4[:, d * LANE:(d + 1) * LANE],
                                       la=g4[:, (2 + d) * LANE:(3 + d) * LANE]))
        for ch, r in zip(chains, _gdn_prep(chains, c, masks)):
            s, a = ch["s"], ch["a"]
            u_scr[s, pl.ds(a, c), :] = r["u"]
            a2 = pl.multiple_of(2 * a, 2 * c)
            wq_scr[s, pl.ds(a2, c), :] = r["w"].astype(wq_scr.dtype)
            wq_scr[s, pl.ds(a2 + c, c), :] = r["qg"].astype(wq_scr.dtype)
            at_scr[s, pl.ds(a, c), :] = r["attn"].astype(at_scr.dtype)
            kd_scr[s, :, pl.ds(a, c)] = r["kd_t"].astype(kd_scr.dtype)
            eg_scr[s, pl.ds(pl.multiple_of(ch["ci"] * 8, 8), 8), :] = jnp.broadcast_to(r["egl"], (8, LANE))
        return 0

    lax.fori_loop(0, n // pcs, prep, 0)

    def body(i, carry):
        cis = [i if s % 2 == 0 else n - 1 - i for s in range(2 * hb)]
        starts = [pl.multiple_of(ci * c, c) for ci in cis]
        ws = [jnp.dot(wq_scr[s, pl.ds(pl.multiple_of(2 * starts[s], 2 * c), 2 * c), :], _mx(carry[s]),
                      preferred_element_type=F32) for s in range(2 * hb)]
        v_new = [_mx(u_scr[s, pl.ds(starts[s], c), :] - ws[s][:c]) for s in range(2 * hb)]
        out = []
        for s in range(2 * hb):
            o_scr[s, pl.ds(starts[s], c), :] = ws[s][c:] + jnp.dot(
                at_scr[s, pl.ds(starts[s], c), :], v_new[s], preferred_element_type=F32)
        for s in range(2 * hb):
            egl = eg_scr[s, pl.ds(pl.multiple_of(cis[s] * 8, 8), 1), :]
            out.append(carry[s] * egl + jnp.dot(kd_scr[s, :, pl.ds(starts[s], c)], v_new[s],
                                                preferred_element_type=F32))
        return tuple(out)

    init = []
    for j in range(hb):
        init += [s0f_ref[0, j], s0b_ref[0, j]]
    fin = lax.fori_loop(0, n, body, tuple(init))
    for j in range(hb):
        sf_ref[0, j] = fin[2 * j]
        sb_ref[0, j] = fin[2 * j + 1]

    def epi(i, _):
        a = pl.multiple_of(i * c, c)
        for j in range(hb):
            ln = slice(j * V_B, (j + 1) * V_B)
            o = o_scr[2 * j, pl.ds(a, c), :] + o_scr[2 * j + 1, pl.ds(a, c), :]
            y = o * lax.rsqrt(jnp.mean(o * o, axis=-1, keepdims=True) + 1e-6) * ng_ref[...]
            y_ref[pl.ds(a, c), ln] = (y * _silu(z_ref[pl.ds(a, c), ln])).astype(y_ref.dtype)
        return 0

    lax.fori_loop(0, n, epi, 0)


def _gdn(p, ps, b, t, conv_w, ng, s0f, s0b):
    c, hb = CHUNK_B, HEADS_B
    wblk = hb * K_B
    base = (2 * WA_QK + 2 * WA_V) // wblk
    ng_ = H_B // hb
    q_off, k_off, v_off, z_off = base, base + ng_, base + 2 * ng_, base + 3 * ng_
    sm_blk = AB_MAIN // LANE
    st_shape = jax.ShapeDtypeStruct((b, H_B, K_B, V_B), F32)
    st_spec = pl.BlockSpec((1, hb, K_B, V_B), lambda i, h: (i, h, 0, 0))
    row_spec = pl.BlockSpec((1, LANE), lambda i, h: (0, 0))
    ns = 2 * hb
    return pl.pallas_call(
        functools.partial(_gdn_kernel, t=t, c=c, hb=hb),
        out_shape=(jax.ShapeDtypeStruct((b * t, WB), MXU_DTYPE), st_shape, st_shape),
        grid=(b, ng_),
        in_specs=[pl.BlockSpec((t, wblk), lambda i, h: (i, q_off + h)),
                  pl.BlockSpec((t, wblk), lambda i, h: (i, k_off + h)),
                  pl.BlockSpec((t, wblk), lambda i, h: (i, v_off + h)),
                  pl.BlockSpec((t, wblk), lambda i, h: (i, z_off + h)),
                  pl.BlockSpec((t, LANE), lambda i, h: (i, 0)),
                  pl.BlockSpec((3, wblk), lambda i, h: (0, h)),
                  pl.BlockSpec((3, wblk), lambda i, h: (0, ng_ + h)),
                  pl.BlockSpec((3, wblk), lambda i, h: (0, 2 * ng_ + h)),
                  row_spec, st_spec, st_spec],
        out_specs=(pl.BlockSpec((t, wblk), lambda i, h: (i, h)), st_spec, st_spec),
        scratch_shapes=[pltpu.VMEM((ns, t, V_B), F32),
                        pltpu.VMEM((ns, 2 * t, K_B), MXU_DTYPE),
                        pltpu.VMEM((ns, t, c), MXU_DTYPE),
                        pltpu.VMEM((ns, K_B, t), MXU_DTYPE),
                        pltpu.VMEM((ns, (t // c) * 8, LANE), F32),
                        pltpu.VMEM((ns, t, V_B), F32)],
        compiler_params=_cparams(("arbitrary", "arbitrary")),
        name="gdn_scan",
    )(p, p, p, p, ps, conv_w, conv_w, conv_w, ng, s0f, s0b)


def _rotary(x, cos, sin_signed):
    even = (_iota2(x.shape, 1) % 2) == 0
    swapped = jnp.where(even, pltpu.roll(x, LANE - 1, 1), pltpu.roll(x, 1, 1))
    return x * cos + swapped * sin_signed


def _ret_kernel(q_ref, k_ref, v_ref, z_ref, cos_ref, sin_ref, ng_ref, nb_ref, s0f_ref, s0b_ref,
                y_ref, sf_ref, sb_ref, of_scr, ob_scr, *, t, c, on_grid):
    n = t // c
    h = pl.program_id(1)
    hf = jnp.full((1, 1), h, jnp.int32).astype(F32)
    lg_f = jnp.log1p(-jnp.exp2(-5.0 - hf))
    lg_b = jnp.log1p(-jnp.exp2(-5.0 - (H_C - 1.0 - hf)))
    row, col = _iota2((c, c), 0), _iota2((c, c), 1)
    dmat_f = jnp.where(col <= row, jnp.exp((row - col).astype(F32) * lg_f), 0.0)
    dmat_b = jnp.where(col >= row, jnp.exp((col - row).astype(F32) * lg_b), 0.0)
    r1 = _iota2((c, 1), 0).astype(F32)
    eg_f, ed_f = jnp.exp((r1 + 1.0) * lg_f), jnp.exp((c - 1.0 - r1) * lg_f)
    eg_b, ed_b = jnp.exp((c - r1) * lg_b), jnp.exp(r1 * lg_b)
    tot_f, tot_b = jnp.exp(c * lg_f), jnp.exp(c * lg_b)

    def load_qk(start):
        q = q_ref[pl.ds(start, c), :]
        k = k_ref[pl.ds(start, c), :] * (K_C ** -0.5)
        if on_grid:
            cs, sn = cos_ref[pl.ds(start, c), :], sin_ref[pl.ds(start, c), :]
            q, k = _rotary(q, cs, sn), _rotary(k, cs, sn)
        return q, k

    def body(i, carry):
        st_f, st_b = carry
        a = pl.multiple_of(i * c, c)
        b = pl.multiple_of((n - 1 - i) * c, c)
        (qf, kf), (qb, kb) = load_qk(a), load_qk(b)
        vf, vb = v_ref[pl.ds(a, c), :], v_ref[pl.ds(b, c), :]
        sc_f, sc_b = _mm_nt(qf, kf), _mm_nt(qb, kb)
        in_f, in_b = _mm(qf * eg_f, st_f), _mm(qb * eg_b, st_b)
        up_f, up_b = _mm_tn(kf * ed_f, vf), _mm_tn(kb * ed_b, vb)
        of_scr[pl.ds(a, c), :] = _mm(sc_f * dmat_f, vf) + in_f
        ob_scr[pl.ds(b, c), :] = _mm(sc_b * dmat_b, vb) + in_b
        return st_f * tot_f + up_f, st_b * tot_b + up_b

    st_f, st_b = lax.fori_loop(0, n, body, (s0f_ref[0, 0], s0b_ref[0, 0]))
    sf_ref[0, 0] = st_f
    sb_ref[0, 0] = st_b

    def epi(i, _):
        a = pl.multiple_of(i * c, c)
        o = of_scr[pl.ds(a, c), :] + ob_scr[pl.ds(a, c), :]
        mu = jnp.mean(o, axis=-1, keepdims=True)
        d = o - mu
        var = jnp.mean(d * d, axis=-1, keepdims=True)
        y = d * lax.rsqrt(var + 1e-5) * ng_ref[...] + nb_ref[...]
        y_ref[pl.ds(a, c), :] = (y * _silu(z_ref[pl.ds(a, c), :])).astype(y_ref.dtype)
        return 0

    lax.fori_loop(0, n, epi, 0)


def _ret(p, b, t, cos_t, sin_t, ng, nb, s0f, s0b, on_grid):
    c = min(CHUNK_C, t)
    q_off, k_off = 0, WC_QK // K_C
    v_off, z_off = (2 * WC_QK) // V_C, (2 * WC_QK + WC_V) // V_C
    st_shape = jax.ShapeDtypeStruct((b, H_C, K_C, V_C), F32)
    st_spec = pl.BlockSpec((1, 1, K_C, V_C), lambda i, h: (i, h, 0, 0))
    tab_spec = pl.BlockSpec((t, K_C), lambda i, h: (0, 0))
    vrow = pl.BlockSpec((1, V_C), lambda i, h: (0, 0))
    return pl.pallas_call(
        functools.partial(_ret_kernel, t=t, c=c, on_grid=on_grid),
        out_shape=(jax.ShapeDtypeStruct((b * t, WC_V), MXU_DTYPE), st_shape, st_shape),
        grid=(b, H_C),
        in_specs=[pl.BlockSpec((t, K_C), lambda i, h: (i, q_off + h)),
                  pl.BlockSpec((t, K_C), lambda i, h: (i, k_off + h)),
                  pl.BlockSpec((t, V_C), lambda i, h: (i, v_off + h)),
                  pl.BlockSpec((t, V_C), lambda i, h: (i, z_off + h)),
                  tab_spec, tab_spec, vrow, vrow, st_spec, st_spec],
        out_specs=(pl.BlockSpec((t, V_C), lambda i, h: (i, h)), st_spec, st_spec),
        scratch_shapes=[pltpu.VMEM((t, V_C), F32), pltpu.VMEM((t, V_C), F32)],
        compiler_params=_cparams(("arbitrary", "arbitrary")),
        name="ret_scan",
    )(p, p, p, p, cos_t, sin_t, ng, nb, s0f, s0b)


def _ssd_chunks(chains, states, consts, expands, c):
    hp = HEADS_PER_GROUP
    w = hp * P_D
    gates = [_sel2_r(ch["dt_la"], expands[ch["rev"]]) for ch in chains]
    for ch, gt in zip(chains, gates):
        ch["v"], ch["la_x"] = ch["x"] * gt[:, :w], gt[:, w:]
    cums = [_sel2_l(consts[ch["rev"]][0], jnp.concatenate([ch["la_x"] * consts[ch["rev"]][1], ch["la_x"]], axis=1))
            for ch in chains]
    scores = [_mm_nt(ch["cm"], jnp.concatenate([ch["bd"]] * hp, axis=0)) for ch in chains]
    for ch, x, sc in zip(chains, cums, scores):
        gx = x[:, w:]
        ch["gx"], ch["gl"] = gx, (gx[0:1] if ch["rev"] else gx[c - 1:c])
        ch["p"] = sc * jnp.exp(x[:, :w]) * consts[ch["rev"]][2]
    intra = [_mm(ch["p"], jnp.concatenate([ch["v"]] * hp, axis=0) * consts[ch["rev"]][3]) for ch in chains]
    upd = [_mm_tn(ch["bd"], ch["v"] * jnp.exp(ch["gl"] - ch["gx"])) for ch in chains]
    outs, states = [], dict(states)
    for ch, oi, up in zip(chains, intra, upd):
        st = states[ch["rev"]]
        outs.append(oi + _mm(ch["cm"], st) * jnp.exp(ch["gx"]))
        states[ch["rev"]] = st * jnp.exp(ch["gl"]) + up
    return outs, states


def _ssd_kernel(x_ref, b_ref, c_ref, z_ref, sm_ref, wx_ref, wb_ref, wc_ref, bx_ref, bb_ref, bc_ref,
                dsk_ref, ng_ref, s0f_ref, s0b_ref,
                y_ref, sf_ref, sb_ref, xs, bs, cs, of_scr, ob_scr, *, t, c):
    n = t // c
    hp = HEADS_PER_GROUP
    w = hp * P_D
    grp = pl.program_id(1)
    erow, ecol = _iota2((LANE, 2 * w), 0), _iota2((LANE, 2 * w), 1)
    head = grp * hp + (ecol % w) // P_D + jnp.where(ecol < w, 0, CD_LA)
    expand_f = _f01(erow == CD_DTF + head)
    expand_b = _f01(erow == CD_DTB + head)

    def prep(i, _):
        a = pl.multiple_of(i * c, c)
        xs[pl.ds(a, c), :] = _silu(_conv3_chunk(x_ref, wx_ref, a, c, t, bx_ref[...]))
        bs[pl.ds(a, c), :] = _silu(_conv3_chunk(b_ref, wb_ref, a, c, t, bb_ref[...]))
        cs[pl.ds(a, c), :] = _silu(_conv3_chunk(c_ref, wc_ref, a, c, t, bc_ref[...]))
        return 0

    lax.fori_loop(0, n, prep, 0)

    row, col = _iota2((c, c), 0), _iota2((c, c), 1)
    rt, ct = _iota2((c, hp * c), 0), _iota2((c, hp * c), 1) % c
    brow, bcol = _iota2((hp * c, w), 0), _iota2((hp * c, w), 1)
    bd_mask = _f01((brow // c) == (bcol // P_D))
    consts = {False: (_f01(col <= row), _f01(rt > ct), _f01(ct <= rt), bd_mask),
              True: (_f01(col >= row), _f01(rt < ct), _f01(ct >= rt), bd_mask)}
    expands = {False: expand_f, True: expand_b}
    o_scrs = {False: of_scr, True: ob_scr}

    def body(i, carry):
        chains = []
        for cc in range(SSD_CHUNKS):
            for rev in (False, True):
                ci = i * SSD_CHUNKS + cc
                a = pl.multiple_of((n - 1 - ci if rev else ci) * c, c)
                chains.append(dict(a=a, rev=rev, cm=cs[pl.ds(a, c), :], bd=bs[pl.ds(a, c), :],
                                   x=xs[pl.ds(a, c), :], dt_la=sm_ref[pl.ds(a, c), :]))
        outs, states = _ssd_chunks(chains, {False: carry[0], True: carry[1]}, consts, expands, c)
        for ch, o in zip(chains, outs):
            o_scrs[ch["rev"]][pl.ds(ch["a"], c), :] = o
        return states[False], states[True]

    st_f, st_b = lax.fori_loop(0, n // SSD_CHUNKS, body, (s0f_ref[0, 0], s0b_ref[0, 0]))
    sf_ref[0, 0] = st_f
    sb_ref[0, 0] = st_b

    def epi(i, _):
        a = pl.multiple_of(i * c, c)
        o = of_scr[pl.ds(a, c), :] + ob_scr[pl.ds(a, c), :] + dsk_ref[...] * xs[pl.ds(a, c), :]
        o = o * _silu(z_ref[pl.ds(a, c), :])
        y = o * lax.rsqrt(jnp.mean(o * o, axis=-1, keepdims=True) + 1e-6) * ng_ref[...]
        y_ref[pl.ds(a, c), :] = y.astype(y_ref.dtype)
        return 0

    lax.fori_loop(0, n, epi, 0)


def _ssd(p, ps, b, t, conv_w, conv_b, dskip, ng, s0f, s0b):
    c = CHUNK_D
    w = HEADS_PER_GROUP * P_D
    base = 2 * WC_QK + 2 * WC_V
    x_off = base // w
    b_off = (base + WD) // N_D
    c_off = (base + WD + WD_BC) // N_D
    z_off = (base + WD + 2 * WD_BC) // w
    sm_blk = CD_MAIN // LANE
    st_shape = jax.ShapeDtypeStruct((b, G_D, N_D, w), F32)
    st_spec = pl.BlockSpec((1, 1, N_D, w), lambda i, g: (i, g, 0, 0))
    row_spec = pl.BlockSpec((1, LANE), lambda i, g: (0, 0))
    return pl.pallas_call(
        functools.partial(_ssd_kernel, t=t, c=c),
        out_shape=(jax.ShapeDtypeStruct((b * t, WD), MXU_DTYPE), st_shape, st_shape),
        grid=(b, G_D),
        in_specs=[pl.BlockSpec((t, w), lambda i, g: (i, x_off + g)),
                  pl.BlockSpec((t, N_D), lambda i, g: (i, b_off + g)),
                  pl.BlockSpec((t, N_D), lambda i, g: (i, c_off + g)),
                  pl.BlockSpec((t, w), lambda i, g: (i, z_off + g)),
                  pl.BlockSpec((t, LANE), lambda i, g: (i, 0)),
                  pl.BlockSpec((3, w), lambda i, g: (0, g)),
                  pl.BlockSpec((3, N_D), lambda i, g: (0, WD // N_D + g)),
                  pl.BlockSpec((3, N_D), lambda i, g: (0, (WD + WD_BC) // N_D + g)),
                  pl.BlockSpec((1, w), lambda i, g: (0, g)),
                  pl.BlockSpec((1, N_D), lambda i, g: (0, WD // N_D + g)),
                  pl.BlockSpec((1, N_D), lambda i, g: (0, (WD + WD_BC) // N_D + g)),
                  pl.BlockSpec((1, w), lambda i, g: (0, g)),
                  pl.BlockSpec((1, w), lambda i, g: (0, g)),
                  st_spec, st_spec],
        out_specs=(pl.BlockSpec((t, w), lambda i, g: (i, g)), st_spec, st_spec),
        scratch_shapes=[pltpu.VMEM((t, w), F32), pltpu.VMEM((t, N_D), F32), pltpu.VMEM((t, N_D), F32),
                        pltpu.VMEM((t, w), F32), pltpu.VMEM((t, w), F32)],
        compiler_params=_cparams(("arbitrary", "arbitrary")),
        name="ssd_scan",
    )(p, p, p, p, ps, conv_w, conv_w, conv_w, conv_b, conv_b, conv_b, dskip, ng, s0f, s0b)


def _split_cols(w, sizes):
    out, o = [], 0
    for s in sizes:
        out.append(w[..., o:o + s])
        o += s
    return out


def _pad_cols(w, n):
    return jnp.pad(w, [(0, 0)] * (w.ndim - 1) + [(0, n - w.shape[-1])])


def _lane_row(pieces):
    row = jnp.zeros((LANE,), F32)
    for off, vec in pieces:
        row = lax.dynamic_update_slice(row, vec.astype(F32), (off,))
    return row.reshape(1, LANE)


def _rotary_tables(t):
    n_rows = t // GRID_W
    rows = jnp.repeat(jnp.arange(n_rows), GRID_W).astype(F32)
    cols = jnp.tile(jnp.arange(GRID_W), n_rows).astype(F32)
    n_freq = K_C // 4
    inv_freq = ROPE_BASE ** (-jnp.arange(n_freq, dtype=F32) / n_freq)
    ang = jnp.concatenate([rows[:, None] * inv_freq, cols[:, None] * inv_freq], axis=-1)
    cos = jnp.repeat(jnp.cos(ang), 2, axis=-1)
    sin = jnp.repeat(jnp.sin(ang), 2, axis=-1)
    sign = jnp.tile(jnp.array([-1.0, 1.0], F32), K_C // 2)
    return cos, sin * sign


def kernel(x_prompt, x_sample, state_gla_fwd, state_gla_bwd, state_gdn_fwd, state_gdn_bwd, state_ret_fwd, state_ret_bwd, state_ssd_fwd, state_ssd_bwd, c, c_ctx, ada_w, ada_b, norm_g, final_norm_g, ab_w_in, ab_w_out, gla_gate_w2, gla_gate_b, gla_norm_g, gdn_conv_w, gdn_a_log, gdn_dt_bias, gdn_norm_g, cd_w_in, cd_w_out, ret_norm_g, ret_norm_b, ssd_conv_w, ssd_conv_b, ssd_a_log, ssd_dt_bias, ssd_d, ssd_norm_g):
    bp, tp, _ = x_prompt.shape
    bs, ts, _ = x_sample.shape
    hp = HEADS_PER_GROUP

    cvec = jnp.concatenate([c_ctx[None, :], c], axis=0)
    cvec16 = jnp.pad(cvec, ((0, 16 - cvec.shape[0]), (0, 0)))
    mod = _ada_mod(cvec16, ada_w, ada_b).reshape(DEPTH, 16, 3, D_MODEL)

    groups = [
        dict(x=x_prompt, b=bp, t=tp, shared=True, on_grid=False, rows=slice(0, 1)),
        dict(x=x_sample, b=bs, t=ts, shared=False, on_grid=True, rows=slice(1, 1 + bs)),
    ]
    cos_t, sin_t = _rotary_tables(ts)
    xs = [g["x"].reshape(g["b"] * g["t"], D_MODEL) for g in groups]
    new_states = {}

    for l in range(DEPTH):
        i = l // 2
        if l % 2 == 0:
            w = ab_w_in[i]
            n_a = 2 * WA_QK + 2 * WA_V
            n_b = n_a + 2 * GLA_RANK + 4 * WB
            w_in = jnp.concatenate([w[:, :n_a].astype(MXU_DTYPE),
                                    w[:, n_a + 2 * GLA_RANK:n_b].astype(MXU_DTYPE)], axis=1)
            w_small = jnp.concatenate([w[:, n_a:n_a + 2 * GLA_RANK].astype(MXU_DTYPE), w[:, n_b:].astype(MXU_DTYPE),
                                       jnp.zeros((D_MODEL, LANE - 2 * GLA_RANK - 4 * H_B), MXU_DTYPE)], axis=1)
            w_out = ab_w_out[i].astype(MXU_DTYPE)
            wg = jnp.zeros((2, LANE, WA_QK), F32)
            wg = wg.at[0, AB_RF:AB_RF + GLA_RANK].set(gla_gate_w2[i, 0])
            wg = wg.at[1, AB_RB:AB_RB + GLA_RANK].set(gla_gate_w2[i, 1])
            bg = gla_gate_b[i].reshape(2, 1, WA_QK)
            dtb_row = _lane_row([(AB_AF, gdn_dt_bias[i, 0]), (AB_AB, gdn_dt_bias[i, 1])])
            alog_row = _lane_row([(AB_AF, gdn_a_log[i, 0]), (AB_AB, gdn_a_log[i, 1])])
        else:
            w = cd_w_in[i]
            w_in = w[:, :CD_MAIN].astype(MXU_DTYPE)
            w_small = jnp.concatenate([w[:, CD_MAIN:].astype(MXU_DTYPE),
                                       jnp.zeros((D_MODEL, LANE - 2 * H_D), MXU_DTYPE)], axis=1)
            w_out = cd_w_out[i].astype(MXU_DTYPE)
            dtb_row = _lane_row([(CD_DTF, ssd_dt_bias[i, 0]), (CD_DTB, ssd_dt_bias[i, 1])])
            alog_row = _lane_row([(CD_DTF, ssd_a_log[i, 0]), (CD_DTB, ssd_a_log[i, 1])])
            dskip = jnp.repeat(ssd_d[i], P_D).reshape(1, WD)

        for gi, g in enumerate(groups):
            b, t = g["b"], g["t"]
            mod_g = mod[l, g["rows"]]
            h = _norm_mod(xs[gi].reshape(b, t, D_MODEL), norm_g[l], mod_g, g["shared"])
            p, ps = _proj(h, w_in, w_small, dtb_row, alog_row, "ab" if l % 2 == 0 else "cd")
            if l % 2 == 0:
                if gi == 0:
                    sa_f = sa_b = jnp.zeros((b, H_A, V_A, K_A), F32)
                    sb_f = sb_b = jnp.zeros((b, H_B, K_B, V_B), F32)
                else:
                    sa_f = jnp.swapaxes(state_gla_fwd[:, i], -1, -2)
                    sa_b = jnp.swapaxes(state_gla_bwd[:, i], -1, -2)
                    sb_f, sb_b = state_gdn_fwd[:, i], state_gdn_bwd[:, i]
                y1, fa_f, fa_b = _gla(p, ps, b, t, wg, bg, gla_norm_g[i].reshape(1, V_A), sa_f, sa_b)
                y2, fb_f, fb_b = _gdn(p, ps, b, t, gdn_conv_w[i], gdn_norm_g[i].reshape(1, V_B), sb_f, sb_b)
                if gi == 0:
                    new_states["gla_f"] = jnp.swapaxes(fa_f, -1, -2)
                    new_states["gla_b"] = jnp.swapaxes(fa_b, -1, -2)
                    new_states["gdn_f"], new_states["gdn_b"] = fb_f, fb_b
            else:
                if gi == 0:
                    sc_f = sc_b = jnp.zeros((b, H_C, K_C, V_C), F32)
                    sd_f = sd_b = jnp.zeros((b, G_D, N_D, hp * P_D), F32)
                else:
                    sc_f, sc_b = state_ret_fwd[:, i], state_ret_bwd[:, i]

                    def to_grp(s):
                        s = s.reshape(b, G_D, hp, N_D, P_D)
                        return jnp.transpose(s, (0, 1, 3, 2, 4)).reshape(b, G_D, N_D, hp * P_D)

                    sd_f, sd_b = to_grp(state_ssd_fwd[:, i]), to_grp(state_ssd_bwd[:, i])
                tab_c, tab_s = (cos_t, sin_t) if g["on_grid"] else (cos_t[:t], sin_t[:t])
                y1, fc_f, fc_b = _ret(p, b, t, tab_c, tab_s, ret_norm_g[i].reshape(1, V_C),
                                      ret_norm_b[i].reshape(1, V_C), sc_f, sc_b, g["on_grid"])
                y2, fd_f, fd_b = _ssd(p, ps, b, t, ssd_conv_w[i], ssd_conv_b[i].reshape(1, -1),
                                      dskip, ssd_norm_g[i].reshape(1, WD), sd_f, sd_b)
                if gi == 0:
                    def from_grp(s):
                        s = s.reshape(b, G_D, N_D, hp, P_D)
                        return jnp.transpose(s, (0, 1, 3, 2, 4)).reshape(b, H_D, N_D, P_D)

                    new_states["ret_f"], new_states["ret_b"] = fc_f, fc_b
                    new_states["ssd_f"], new_states["ssd_b"] = from_grp(fd_f), from_grp(fd_b)
            xs[gi] = _out_proj(y1, y2, w_out, xs[gi], mod_g, t, g["shared"])

    y_prompt = _final_norm(xs[0], final_norm_g).reshape(bp, tp, D_MODEL)
    y_sample = _final_norm(xs[1], final_norm_g).reshape(bs, ts, D_MODEL)
    st = lambda k: new_states[k][:, None]
    return (y_prompt, y_sample, st("gla_f"), st("gla_b"), st("gdn_f"), st("gdn_b"),
            st("ret_f"), st("ret_b"), st("ssd_f"), st("ssd_b"))
```

```python
import functools
import math

import jax
import jax.numpy as jnp
from jax import lax
from jax.experimental import pallas as pl
from jax.experimental.pallas import tpu as pltpu

F32 = jnp.float32
MXU_DTYPE = jnp.bfloat16

D_MODEL = 4096
DEPTH = 2
GRID_W = 64
ROPE_BASE = 10000.0
H_A, K_A, V_A = 8, 128, 256
GLA_RANK = 16
GLA_GATE_NORM = 16.0
H_B, K_B, V_B = 16, 128, 128
H_C, K_C, V_C = 8, 128, 256
H_D, N_D, P_D, G_D = 32, 128, 64, 4
HEADS_PER_GROUP = H_D // G_D
WA_QK, WA_V = H_A * K_A, H_A * V_A
WB = H_B * K_B
WC_QK, WC_V = H_C * K_C, H_C * V_C
WD, WD_BC = H_D * P_D, G_D * N_D
AB_SIZES = (WA_QK, WA_QK, WA_V, WA_V, GLA_RANK, GLA_RANK, 3 * WB, WB, H_B, H_B, H_B, H_B)
CD_SIZES = (WC_QK, WC_QK, WC_V, WC_V, WD + 2 * WD_BC, WD, H_D, H_D)

LANE = 128
VMEM_LIMIT = 52 * 1024 * 1024

AB_MAIN = 2 * WA_QK + 2 * WA_V + 4 * WB
CD_MAIN = 2 * WC_QK + 2 * WC_V + (WD + 2 * WD_BC) + WD
TN_PROJ = 1024
AB_RF, AB_RB, AB_BF, AB_BB, AB_AF, AB_AB = 0, 16, 32, 48, 64, 80
CD_DTF, CD_DTB, CD_LA = 0, 32, 64

CHUNK_A = 64
SUB_A = 16
HEADS_A = 2
GLA_CHUNKS = 4
CHUNK_B = 128
HEADS_B = 2
PREP_CHUNKS_B = 4
INV_BASE = 8
CHUNK_C = 256
CHUNK_D = P_D
SSD_CHUNKS = 4


def _mx(x):
    return x.astype(MXU_DTYPE)


def _mm(a, b):
    return jnp.dot(_mx(a), _mx(b), preferred_element_type=F32)


def _mm_nt(a, b):
    return lax.dot_general(_mx(a), _mx(b), (((1,), (1,)), ((), ())), preferred_element_type=F32)


def _mm_tn(a, b):
    return lax.dot_general(_mx(a), _mx(b), (((0,), (0,)), ((), ())), preferred_element_type=F32)


def _split3(x):
    hi = _mx(x)
    r1 = x - hi.astype(F32)
    mid = _mx(r1)
    lo = _mx(r1 - mid.astype(F32))
    return hi, mid, lo


def _sel_l(m01, x):
    hi, mid, lo = _split3(x)
    m = _mx(m01)
    d = functools.partial(jnp.dot, preferred_element_type=F32)
    return d(m, hi) + d(m, mid) + d(m, lo)


def _sel_r(x, m01):
    hi, mid, lo = _split3(x)
    m = _mx(m01)
    d = functools.partial(jnp.dot, preferred_element_type=F32)
    return d(hi, m) + d(mid, m) + d(lo, m)


def _sel2_l(m01, x):
    hi = _mx(x)
    lo = _mx(x - hi.astype(F32))
    m = _mx(m01)
    return jnp.dot(m, hi, preferred_element_type=F32) + jnp.dot(m, lo, preferred_element_type=F32)


def _sel2_r(x, m01):
    hi = _mx(x)
    lo = _mx(x - hi.astype(F32))
    m = _mx(m01)
    return jnp.dot(hi, m, preferred_element_type=F32) + jnp.dot(lo, m, preferred_element_type=F32)


def _mm3(a, b):
    a_hi = _mx(a)
    a_lo = _mx(a - a_hi.astype(F32))
    b_hi = _mx(b)
    b_lo = _mx(b - b_hi.astype(F32))
    d = functools.partial(jnp.dot, preferred_element_type=F32)
    return d(a_hi, b_hi) + d(a_hi, b_lo) + d(a_lo, b_hi)


def _silu(x):
    return x * (1.0 / (1.0 + jnp.exp(-x)))


def _sigmoid(x):
    return 1.0 / (1.0 + jnp.exp(-x))


def _softplus(x):
    return jnp.maximum(x, 0.0) + jnp.log1p(jnp.exp(-jnp.abs(x)))


def _log_sigmoid(x):
    return jnp.minimum(x, 0.0) - jnp.log1p(jnp.exp(-jnp.abs(x)))


def _iota2(shape, dim):
    return lax.broadcasted_iota(jnp.int32, shape, dim)


def _f01(mask):
    return jnp.where(mask, 1.0, 0.0).astype(F32)


def _shift_rows(x, prev_row, next_row):
    c = x.shape[0]
    r = _iota2(x.shape, 0)
    xp = jnp.where(r == 0, prev_row, pltpu.roll(x, 1, 0))
    xn = jnp.where(r == c - 1, next_row, pltpu.roll(x, c - 1, 0))
    return xp, xn


def _conv3_chunk(ref, w_ref, start, c, t_total, bias=None, lanes=slice(None)):
    x = ref[pl.ds(start, c), lanes]
    lo = pl.multiple_of(jnp.maximum(start - 8, 0), 8)
    hi = pl.multiple_of(jnp.minimum(start + c, t_total - 8), 8)
    prev = ref[pl.ds(lo, 8), lanes][7:8] * (start > 0).astype(F32)
    nxt = ref[pl.ds(hi, 8), lanes][0:1] * (start + c < t_total).astype(F32)
    xp, xn = _shift_rows(x, prev, nxt)
    y = xp * w_ref[0:1, lanes] + x * w_ref[1:2, lanes] + xn * w_ref[2:3, lanes]
    if bias is not None:
        y = y + bias
    return y


def _aligned(x, c):
    return x if isinstance(x, int) else pl.multiple_of(x, c)


def _run_trips(m, body, init):
    if m == 1:
        return body(0, init, "same")
    assert m % 2 == 0
    carry = lax.fori_loop(0, m // 2, lambda i, cr: body(i, cr, None), init)
    return lax.fori_loop(m // 2, m, lambda i, cr: body(i, cr, "both"), carry)


def _finish_visited(starts, done, finish):
    for a, d in starts:
        if done == "both" or (done == "same" and d == 0):
            finish(a)


def _cparams(sem):
    return pltpu.CompilerParams(dimension_semantics=sem, vmem_limit_bytes=VMEM_LIMIT)


def _ada_kernel(c_ref, w_ref, b_ref, o_ref):
    a = _silu(c_ref[...])
    o_ref[0] = _mm3(a, w_ref[0]) + b_ref[0]


def _ada_mod(cvec16, ada_w, ada_b):
    tn = 512
    n3 = ada_w.shape[-1]
    return pl.pallas_call(
        _ada_kernel,
        out_shape=jax.ShapeDtypeStruct((DEPTH, 16, n3), F32),
        grid=(DEPTH, n3 // tn),
        in_specs=[pl.BlockSpec((16, D_MODEL), lambda l, j: (0, 0)),
                  pl.BlockSpec((1, D_MODEL, tn), lambda l, j: (l, 0, j)),
                  pl.BlockSpec((1, 1, tn), lambda l, j: (l, 0, j))],
        out_specs=pl.BlockSpec((1, 16, tn), lambda l, j: (l, 0, j)),
        compiler_params=_cparams(("arbitrary", "arbitrary")),
        name="ada_mod",
    )(cvec16, ada_w, ada_b.reshape(DEPTH, 1, n3))


def _norm_mod_kernel(x_ref, g_ref, m_ref, o_ref):
    x = x_ref[0]
    y = x * lax.rsqrt(jnp.mean(x * x, axis=-1, keepdims=True) + 1e-6)
    y = y * g_ref[...]
    o_ref[...] = (y * (1.0 + m_ref[0, 1:2, :]) + m_ref[0, 0:1, :]).astype(o_ref.dtype)


def _norm_mod(x, g, mod, shared):
    b, t, d = x.shape
    tt = 256
    nt = t // tt
    mmap = (lambda i, j: (0, 0, 0)) if shared else (lambda i, j: (i, 0, 0))
    return pl.pallas_call(
        _norm_mod_kernel,
        out_shape=jax.ShapeDtypeStruct((b * t, d), MXU_DTYPE),
        grid=(b, nt),
        in_specs=[pl.BlockSpec((1, tt, d), lambda i, j: (i, j, 0)),
                  pl.BlockSpec((1, d), lambda i, j: (0, 0)),
                  pl.BlockSpec((1, 3, d), mmap)],
        out_specs=pl.BlockSpec((tt, d), lambda i, j: (i * nt + j, 0)),
        compiler_params=_cparams(("arbitrary", "arbitrary")),
        name="norm_mod",
    )(x, g.reshape(1, d), mod)


def _regroup_kernel(a_ref, b_ref, o_ref, os_ref, *, first_shifted, last, shift, tail):
    j = pl.program_id(1)

    @pl.when(j < first_shifted)
    def _():
        o_ref[...] = a_ref[...].astype(o_ref.dtype)

    @pl.when(j >= first_shifted)
    def _():
        o_ref[...] = jnp.concatenate([a_ref[:, shift:], b_ref[:, :shift]], axis=1).astype(o_ref.dtype)

    @pl.when(j == first_shifted - 1)
    def _():
        os_ref[:, :shift] = b_ref[:, :shift].astype(os_ref.dtype)

    @pl.when(j == last)
    def _():
        os_ref[:, shift:shift + tail] = b_ref[:, shift:shift + tail].astype(os_ref.dtype)
        os_ref[:, shift + tail:] = jnp.zeros((os_ref.shape[0], LANE - shift - tail), os_ref.dtype)


def _regroup_ab_weight(w):
    d = w.shape[0]
    n_a = 2 * WA_QK + 2 * WA_V
    shift, tail = 2 * GLA_RANK, 4 * H_B
    tr, tc = 1024, 1024
    assert n_a % tc == 0 and AB_MAIN % tc == 0 and shift + tail <= LANE
    assert w.shape[1] == AB_MAIN + shift + tail
    nj = AB_MAIN // tc
    return pl.pallas_call(
        functools.partial(_regroup_kernel, first_shifted=n_a // tc, last=nj - 1, shift=shift, tail=tail),
        out_shape=(jax.ShapeDtypeStruct((d, AB_MAIN), MXU_DTYPE), jax.ShapeDtypeStruct((d, LANE), MXU_DTYPE)),
        grid=(d // tr, nj),
        in_specs=[pl.BlockSpec((tr, tc), lambda r, j: (r, j)),
                  pl.BlockSpec((tr, LANE), lambda r, j: (r, (j + 1) * (tc // LANE)))],
        out_specs=(pl.BlockSpec((tr, tc), lambda r, j: (r, j)),
                   pl.BlockSpec((tr, LANE), lambda r, j: (r, 0))),
        compiler_params=_cparams(("arbitrary", "arbitrary")),
        name="regroup_w",
    )(w, w)


def _gate_lanes(x, dtb, alog, kind):
    lane = _iota2((1, LANE), 1)
    neg_a = -jnp.exp(alog)
    if kind == "ab":
        return jnp.where(lane < AB_BF, x, jnp.where(lane < AB_AF, _sigmoid(x), neg_a * _softplus(x + dtb)))
    dt = _softplus(x + dtb)
    return jnp.where(lane < CD_LA, dt, pltpu.roll(neg_a * dt, CD_LA, 1))


def _proj_kernel(h_ref, w_ref, ws_ref, dtb_ref, alog_ref, o_ref, os_ref, *, kind):
    o_ref[...] = jnp.dot(h_ref[...], w_ref[...], preferred_element_type=F32)

    @pl.when(pl.program_id(1) == 0)
    def _():
        x = jnp.dot(h_ref[...], ws_ref[...], preferred_element_type=F32)
        os_ref[...] = _gate_lanes(x, dtb_ref[...], alog_ref[...], kind)


def _proj(h, w, w_small, dtb_row, alog_row, kind):
    m, d = h.shape
    n = AB_MAIN if kind == "ab" else CD_MAIN
    assert w.shape[1] >= n and n % TN_PROJ == 0
    tm = min(1024, m)
    row_spec = pl.BlockSpec((1, LANE), lambda i, j: (0, 0))
    return pl.pallas_call(
        functools.partial(_proj_kernel, kind=kind),
        out_shape=(jax.ShapeDtypeStruct((m, n), F32), jax.ShapeDtypeStruct((m, LANE), F32)),
        grid=(m // tm, n // TN_PROJ),
        in_specs=[pl.BlockSpec((tm, d), lambda i, j: (i, 0)),
                  pl.BlockSpec((d, TN_PROJ), lambda i, j: (0, j)),
                  pl.BlockSpec((d, LANE), lambda i, j: (0, 0)),
                  row_spec, row_spec],
        out_specs=(pl.BlockSpec((tm, TN_PROJ), lambda i, j: (i, j)),
                   pl.BlockSpec((tm, LANE), lambda i, j: (i, 0))),
        compiler_params=_cparams(("arbitrary", "arbitrary")),
        name="in_proj",
    )(h, w, w_small, dtb_row, alog_row)


def _out_kernel(y1_ref, y2_ref, w1_ref, w2_ref, x_ref, m_ref, o_ref):
    acc = jnp.dot(y1_ref[...], w1_ref[...], preferred_element_type=F32)
    acc = acc + jnp.dot(y2_ref[...], w2_ref[...], preferred_element_type=F32)
    o_ref[...] = x_ref[...] + m_ref[0, 2:3, :] * acc


def _out_proj(y1, y2, w, x2d, mod, t, shared):
    m, d = x2d.shape
    k1, k2 = y1.shape[1], y2.shape[1]
    assert k1 == k2 and w.shape[0] == k1 + k2
    tm, tn = 1024, 512
    per_b = t // tm if t >= tm else 1
    mmap = (lambda i, j: (0, 0, j)) if shared else (lambda i, j: (i // per_b, 0, j))
    return pl.pallas_call(
        _out_kernel,
        out_shape=jax.ShapeDtypeStruct((m, d), F32),
        grid=(m // tm, d // tn),
        in_specs=[pl.BlockSpec((tm, k1), lambda i, j: (i, 0)),
                  pl.BlockSpec((tm, k2), lambda i, j: (i, 0)),
                  pl.BlockSpec((k1, tn), lambda i, j: (0, j)),
                  pl.BlockSpec((k2, tn), lambda i, j: (1, j)),
                  pl.BlockSpec((tm, tn), lambda i, j: (i, j)),
                  pl.BlockSpec((1, 3, tn), mmap)],
        out_specs=pl.BlockSpec((tm, tn), lambda i, j: (i, j)),
        compiler_params=_cparams(("arbitrary", "arbitrary")),
        name="out_proj",
    )(y1, y2, w, w, x2d, mod)


def _final_norm_kernel(x_ref, g_ref, o_ref):
    x = x_ref[...]
    o_ref[...] = x * lax.rsqrt(jnp.mean(x * x, axis=-1, keepdims=True) + 1e-6) * g_ref[...]


def _final_norm(x2d, g):
    m, d = x2d.shape
    tt = 256
    return pl.pallas_call(
        _final_norm_kernel,
        out_shape=jax.ShapeDtypeStruct((m, d), F32),
        grid=(m // tt,),
        in_specs=[pl.BlockSpec((tt, d), lambda i: (i, 0)),
                  pl.BlockSpec((1, d), lambda i: (0, 0))],
        out_specs=pl.BlockSpec((tt, d), lambda i: (i, 0)),
        compiler_params=_cparams(("arbitrary",)),
        name="final_norm",
    )(x2d, g.reshape(1, d))


def _gla_consts(c, s):
    row, col = _iota2((c, c), 0), _iota2((c, c), 1)
    same = (col // s) == (row // s)
    out = {}
    for rev in (False, True):
        tri = (col >= row) if rev else (col <= row)
        out[rev] = (jnp.concatenate([_f01(tri), _f01(jnp.logical_and(tri, same))], axis=0), _f01(tri))
    return out


def _gla_chunks(chains, states, consts, c, s):
    nb = c // s
    rowi = _iota2((c, LANE), 0)
    cums = [_sel2_l(consts[ch["rev"]][0], ch["l"]) for ch in chains]
    score_blocks = []
    for ch, x in zip(chains, cums):
        g, cb = x[:c], x[c:]
        ch["g"] = g
        ch["gl"] = g[0:1] if ch["rev"] else g[c - 1:c]
        qt = ch["q"] * jnp.exp(cb)
        blocks = []
        for i in range(nb):
            if ch["rev"]:
                g_ref = g[(i + 1) * s:(i + 1) * s + 1] if i < nb - 1 else jnp.zeros((1, LANE), F32)
                valid = rowi >= i * s
            else:
                g_ref = g[i * s - 1:i * s] if i > 0 else jnp.zeros((1, LANE), F32)
                valid = rowi < (i + 1) * s
            kh = ch["k"] * jnp.exp(jnp.where(valid, g_ref - g, -1e30))
            blocks.append(_mm_nt(qt[i * s:(i + 1) * s], kh))
        score_blocks.append(blocks)
    upd = [_mm_tn(ch["v"], ch["k"] * jnp.exp(ch["gl"] - ch["g"])) for ch in chains]
    intra = [_mm(jnp.where(consts[ch["rev"]][1] > 0.5, jnp.concatenate(blocks, axis=0), 0.0), ch["v"])
             for ch, blocks in zip(chains, score_blocks)]
    outs, states = [], list(states)
    for ch, oi, up in zip(chains, intra, upd):
        st = states[ch["slot"]]
        outs.append(oi + _mm_nt(ch["q"] * jnp.exp(ch["g"]), st))
        states[ch["slot"]] = st * jnp.exp(ch["gl"]) + up
    return outs, states


def _gla_kernel(q_ref, k_ref, v_ref, z_ref, sm_ref, wg_ref, bg_ref, ng_ref, s0f_ref, s0b_ref,
                y_ref, sf_ref, sb_ref, o_scr, *, t, c, s, hb):
    n = t // c
    consts = _gla_consts(c, s)

    def finish(a):
        for j in range(hb):
            vl = slice(j * V_A, (j + 1) * V_A)
            o = o_scr[2 * j, pl.ds(a, c), :] + o_scr[2 * j + 1, pl.ds(a, c), :]
            y = o * lax.rsqrt(jnp.mean(o * o, axis=-1, keepdims=True) + 1e-6) * ng_ref[...]
            y_ref[pl.ds(a, c), vl] = (y * _silu(z_ref[pl.ds(a, c), vl])).astype(y_ref.dtype)

    def body(i, carry, done):
        chains, starts = [], []
        for cc in range(GLA_CHUNKS):
            ci = i * GLA_CHUNKS + cc
            for d in range(2):
                starts.append((_aligned((ci if d == 0 else n - 1 - ci) * c, c), d))
            for j in range(hb):
                kl, vl = slice(j * K_A, (j + 1) * K_A), slice(j * V_A, (j + 1) * V_A)
                for a, d in starts[-2:]:
                    logit = _mm(sm_ref[pl.ds(a, c), :], wg_ref[d, :, kl]) + bg_ref[d, :, kl]
                    chains.append(dict(a=a, rev=d == 1, slot=2 * j + d,
                                       l=_log_sigmoid(logit) * (1.0 / GLA_GATE_NORM),
                                       q=q_ref[pl.ds(a, c), kl] * (K_A ** -0.5), k=k_ref[pl.ds(a, c), kl],
                                       v=v_ref[pl.ds(a, c), vl]))
        outs, states = _gla_chunks(chains, carry, consts, c, s)
        for ch, o in zip(chains, outs):
            o_scr[ch["slot"], pl.ds(ch["a"], c), :] = o
        _finish_visited(starts, done, finish)
        return tuple(states)

    init = []
    for j in range(hb):
        init += [s0f_ref[0, j], s0b_ref[0, j]]
    fin = _run_trips(n // GLA_CHUNKS, body, tuple(init))
    for j in range(hb):
        sf_ref[0, j] = fin[2 * j]
        sb_ref[0, j] = fin[2 * j + 1]


def _gla(p, ps, b, t, wg, bg, ng, s0f, s0b):
    c, s, hb = CHUNK_A, SUB_A, HEADS_A
    ngrp = H_A // hb
    wk, wv = hb * K_A, hb * V_A
    q_off, k_off = 0, WA_QK // wk
    v_off, z_off = (2 * WA_QK) // wv, (2 * WA_QK + WA_V) // wv
    sm_blk = AB_MAIN // LANE
    st_shape = jax.ShapeDtypeStruct((b, H_A, V_A, K_A), F32)
    st_spec = pl.BlockSpec((1, hb, V_A, K_A), lambda i, h: (i, h, 0, 0))
    return pl.pallas_call(
        functools.partial(_gla_kernel, t=t, c=c, s=s, hb=hb),
        out_shape=(jax.ShapeDtypeStruct((b * t, WA_V), MXU_DTYPE), st_shape, st_shape),
        grid=(b, ngrp),
        in_specs=[pl.BlockSpec((t, wk), lambda i, h: (i, q_off + h)),
                  pl.BlockSpec((t, wk), lambda i, h: (i, k_off + h)),
                  pl.BlockSpec((t, wv), lambda i, h: (i, v_off + h)),
                  pl.BlockSpec((t, wv), lambda i, h: (i, z_off + h)),
                  pl.BlockSpec((t, LANE), lambda i, h: (i, 0)),
                  pl.BlockSpec((2, LANE, wk), lambda i, h: (0, 0, h)),
                  pl.BlockSpec((2, 1, wk), lambda i, h: (0, 0, h)),
                  pl.BlockSpec((1, V_A), lambda i, h: (0, 0)),
                  st_spec, st_spec],
        out_specs=(pl.BlockSpec((t, wv), lambda i, h: (i, h)), st_spec, st_spec),
        scratch_shapes=[pltpu.VMEM((2 * hb, t, V_A), F32)],
        compiler_params=_cparams(("arbitrary", "arbitrary")),
        name="gla_scan",
    )(p, p, p, p, ps, wg, bg, ng, s0f, s0b)


def _tri_inverse_minus_eye(mats, c, masks):
    nm = [-a * masks["diag"] for a in mats]
    p2 = [_mm(n, n) for n in nm]
    y = [n + p + _mm(n, p) for n, p in zip(nm, p2)]
    p4 = [_mm(p, p) for p in p2]
    y = [yy + p + _mm(yy, p) for yy, p in zip(y, p4)]
    for coupled in masks["levels"]:
        lm = [a * coupled for a in mats]
        m = [l + _mm(l, yy) for l, yy in zip(lm, y)]
        y = [yy - (mm + _mm(yy, mm)) for yy, mm in zip(y, m)]
    return y


def _gdn_masks(c):
    row, col = _iota2((c, c), 0), _iota2((c, c), 1)
    levels, b = [], INV_BASE
    while b < c:
        levels.append(_f01(jnp.logical_and((row // (2 * b)) == (col // (2 * b)), (row // b) != (col // b))))
        b *= 2
    tri = {False: (_f01(col <= row), _f01(row > col), _f01(col <= row), _f01(col < row)),
           True: (_f01(col >= row), _f01(row < col), _f01(col >= row), _f01(col > row))}
    return dict(diag=_f01((row // INV_BASE) == (col // INV_BASE)), levels=levels, tri=tri)


def _gdn_prep(chains, c, masks):
    tri = masks["tri"]
    dg = [_sel2_l(tri[ch["rev"]][0], jnp.concatenate([ch["la"] * tri[ch["rev"]][1], ch["la"]], axis=1))
          for ch in chains]
    mats = []
    for ch, x in zip(chains, dg):
        g = x[:, c:]
        ch["g"], ch["eg"] = g, jnp.exp(g)
        ch["gl"] = g[0:1] if ch["rev"] else g[c - 1:c]
        ch["dec"] = jnp.exp(x[:, :c]) * tri[ch["rev"]][2]
        mats.append(ch["kk"] * ch["beta"] * ch["dec"] * tri[ch["rev"]][3])
    ys = _tri_inverse_minus_eye(mats, c, masks)
    rhs = [jnp.concatenate([ch["vc"] * ch["beta"], ch["kc"] * ch["beta"] * ch["eg"]], axis=1) for ch in chains]
    uw = [r + _mm(y, r) for r, y in zip(rhs, ys)]
    out = []
    for ch, x in zip(chains, uw):
        out.append(dict(u=x[:, :c], w=x[:, c:], attn=ch["qk"] * ch["dec"], qg=ch["qc"] * ch["eg"],
                        kd_t=(ch["kc"] * jnp.exp(ch["gl"] - ch["g"])).T, egl=jnp.exp(ch["gl"])))
    return out


def _l2n(x):
    return x * lax.rsqrt(jnp.sum(x * x, axis=-1, keepdims=True) + 1e-6)


def _gdn_kernel(q_ref, k_ref, v_ref, z_ref, sm_ref, wq_ref, wk_ref, wv_ref, ng_ref,
                s0f_ref, s0b_ref, y_ref, sf_ref, sb_ref,
                u_scr, wq_scr, at_scr, kd_scr, eg_scr, o_scr, *, t, c, hb):
    n = t // c
    grp = pl.program_id(1)
    srow, scol = _iota2((LANE, 4 * LANE), 0), _iota2((LANE, 4 * LANE), 1)
    base = jnp.where(scol < LANE, AB_BF, jnp.where(scol < 2 * LANE, AB_BB,
                                                   jnp.where(scol < 3 * LANE, AB_AF, AB_AB)))
    sels = [_f01(srow == base + (grp * hb + j)) for j in range(hb)]
    masks = _gdn_masks(c)

    pcs = min(PREP_CHUNKS_B, n)

    def prep(i, _):
        chains = []
        for cc in range(pcs):
            ci = i * pcs + cc
            a = pl.multiple_of(ci * c, c)
            gt = sm_ref[pl.ds(a, c), :]
            for j in range(hb):
                ln = slice(j * K_B, (j + 1) * K_B)
                qc = _l2n(_silu(_conv3_chunk(q_ref, wq_ref, a, c, t, lanes=ln))) * (K_B ** -0.5)
                kc = _l2n(_silu(_conv3_chunk(k_ref, wk_ref, a, c, t, lanes=ln)))
                vc = _silu(_conv3_chunk(v_ref, wv_ref, a, c, t, lanes=ln))
                kq = _mm_nt(jnp.concatenate([kc, qc], axis=0), kc)
                g4 = _sel2_r(gt, sels[j])
                for d in range(2):
                    chains.append(dict(a=a, ci=ci, s=2 * j + d, rev=d == 1, kk=kq[:c], qk=kq[c:],
                                       kc=kc, qc=qc, vc=vc, beta=g4[:, d * LANE:(d + 1) * LANE],
                                       la=g4[:, (2 + d) * LANE:(3 + d) * LANE]))
        for ch, r in zip(chains, _gdn_prep(chains, c, masks)):
            s, a = ch["s"], ch["a"]
            u_scr[s, pl.ds(a, c), :] = r["u"]
            a2 = pl.multiple_of(2 * a, 2 * c)
            wq_scr[s, pl.ds(a2, c), :] = r["w"].astype(wq_scr.dtype)
            wq_scr[s, pl.ds(a2 + c, c), :] = r["qg"].astype(wq_scr.dtype)
            at_scr[s, pl.ds(a, c), :] = r["attn"].astype(at_scr.dtype)
            kd_scr[s, :, pl.ds(a, c)] = r["kd_t"].astype(kd_scr.dtype)
            eg_scr[s, pl.ds(pl.multiple_of(ch["ci"] * 8, 8), 8), :] = jnp.broadcast_to(r["egl"], (8, LANE))
        return 0

    lax.fori_loop(0, n // pcs, prep, 0)

    def finish(a):
        for j in range(hb):
            ln = slice(j * V_B, (j + 1) * V_B)
            o = o_scr[2 * j, pl.ds(a, c), :] + o_scr[2 * j + 1, pl.ds(a, c), :]
            y = o * lax.rsqrt(jnp.mean(o * o, axis=-1, keepdims=True) + 1e-6) * ng_ref[...]
            y_ref[pl.ds(a, c), ln] = (y * _silu(z_ref[pl.ds(a, c), ln])).astype(y_ref.dtype)

    def body(i, carry, done):
        cis = [i if s % 2 == 0 else n - 1 - i for s in range(2 * hb)]
        starts = [_aligned(ci * c, c) for ci in cis]
        ws = [jnp.dot(wq_scr[s, pl.ds(_aligned(2 * starts[s], 2 * c), 2 * c), :], _mx(carry[s]),
                      preferred_element_type=F32) for s in range(2 * hb)]
        v_new = [_mx(u_scr[s, pl.ds(starts[s], c), :] - ws[s][:c]) for s in range(2 * hb)]
        out = []
        for s in range(2 * hb):
            o_scr[s, pl.ds(starts[s], c), :] = ws[s][c:] + jnp.dot(
                at_scr[s, pl.ds(starts[s], c), :], v_new[s], preferred_element_type=F32)
        for s in range(2 * hb):
            egl = eg_scr[s, pl.ds(_aligned(cis[s] * 8, 8), 1), :]
            out.append(carry[s] * egl + jnp.dot(kd_scr[s, :, pl.ds(starts[s], c)], v_new[s],
                                                preferred_element_type=F32))
        _finish_visited([(starts[0], 0), (starts[1], 1)], done, finish)
        return tuple(out)

    init = []
    for j in range(hb):
        init += [s0f_ref[0, j], s0b_ref[0, j]]
    fin = _run_trips(n, body, tuple(init))
    for j in range(hb):
        sf_ref[0, j] = fin[2 * j]
        sb_ref[0, j] = fin[2 * j + 1]


def _gdn(p, ps, b, t, conv_w, ng, s0f, s0b):
    c, hb = CHUNK_B, HEADS_B
    wblk = hb * K_B
    base = (2 * WA_QK + 2 * WA_V) // wblk
    ng_ = H_B // hb
    q_off, k_off, v_off, z_off = base, base + ng_, base + 2 * ng_, base + 3 * ng_
    sm_blk = AB_MAIN // LANE
    st_shape = jax.ShapeDtypeStruct((b, H_B, K_B, V_B), F32)
    st_spec = pl.BlockSpec((1, hb, K_B, V_B), lambda i, h: (i, h, 0, 0))
    row_spec = pl.BlockSpec((1, LANE), lambda i, h: (0, 0))
    ns = 2 * hb
    return pl.pallas_call(
        functools.partial(_gdn_kernel, t=t, c=c, hb=hb),
        out_shape=(jax.ShapeDtypeStruct((b * t, WB), MXU_DTYPE), st_shape, st_shape),
        grid=(b, ng_),
        in_specs=[pl.BlockSpec((t, wblk), lambda i, h: (i, q_off + h)),
                  pl.BlockSpec((t, wblk), lambda i, h: (i, k_off + h)),
                  pl.BlockSpec((t, wblk), lambda i, h: (i, v_off + h)),
                  pl.BlockSpec((t, wblk), lambda i, h: (i, z_off + h)),
                  pl.BlockSpec((t, LANE), lambda i, h: (i, 0)),
                  pl.BlockSpec((3, wblk), lambda i, h: (0, h)),
                  pl.BlockSpec((3, wblk), lambda i, h: (0, ng_ + h)),
                  pl.BlockSpec((3, wblk), lambda i, h: (0, 2 * ng_ + h)),
                  row_spec, st_spec, st_spec],
        out_specs=(pl.BlockSpec((t, wblk), lambda i, h: (i, h)), st_spec, st_spec),
        scratch_shapes=[pltpu.VMEM((ns, t, V_B), F32),
                        pltpu.VMEM((ns, 2 * t, K_B), MXU_DTYPE),
                        pltpu.VMEM((ns, t, c), MXU_DTYPE),
                        pltpu.VMEM((ns, K_B, t), MXU_DTYPE),
                        pltpu.VMEM((ns, (t // c) * 8, LANE), F32),
                        pltpu.VMEM((ns, t, V_B), F32)],
        compiler_params=_cparams(("arbitrary", "arbitrary")),
        name="gdn_scan",
    )(p, p, p, p, ps, conv_w, conv_w, conv_w, ng, s0f, s0b)


def _rotary(x, cos, sin_signed):
    even = (_iota2(x.shape, 1) % 2) == 0
    swapped = jnp.where(even, pltpu.roll(x, LANE - 1, 1), pltpu.roll(x, 1, 1))
    return x * cos + swapped * sin_signed


def _ret_kernel(q_ref, k_ref, v_ref, z_ref, cos_ref, sin_ref, ng_ref, nb_ref, s0f_ref, s0b_ref,
                y_ref, sf_ref, sb_ref, of_scr, ob_scr, *, t, c, on_grid):
    n = t // c
    h = pl.program_id(1)
    hf = jnp.full((1, 1), h, jnp.int32).astype(F32)
    lg_f = jnp.log1p(-jnp.exp2(-5.0 - hf))
    lg_b = jnp.log1p(-jnp.exp2(-5.0 - (H_C - 1.0 - hf)))
    row, col = _iota2((c, c), 0), _iota2((c, c), 1)
    dmat_f = jnp.where(col <= row, jnp.exp((row - col).astype(F32) * lg_f), 0.0)
    dmat_b = jnp.where(col >= row, jnp.exp((col - row).astype(F32) * lg_b), 0.0)
    r1 = _iota2((c, 1), 0).astype(F32)
    eg_f, ed_f = jnp.exp((r1 + 1.0) * lg_f), jnp.exp((c - 1.0 - r1) * lg_f)
    eg_b, ed_b = jnp.exp((c - r1) * lg_b), jnp.exp(r1 * lg_b)
    tot_f, tot_b = jnp.exp(c * lg_f), jnp.exp(c * lg_b)

    def load_qk(start):
        q = q_ref[pl.ds(start, c), :]
        k = k_ref[pl.ds(start, c), :] * (K_C ** -0.5)
        if on_grid:
            cs, sn = cos_ref[pl.ds(start, c), :], sin_ref[pl.ds(start, c), :]
            q, k = _rotary(q, cs, sn), _rotary(k, cs, sn)
        return q, k

    def finish(a):
        o = of_scr[pl.ds(a, c), :] + ob_scr[pl.ds(a, c), :]
        mu = jnp.mean(o, axis=-1, keepdims=True)
        d = o - mu
        var = jnp.mean(d * d, axis=-1, keepdims=True)
        y = d * lax.rsqrt(var + 1e-5) * ng_ref[...] + nb_ref[...]
        y_ref[pl.ds(a, c), :] = (y * _silu(z_ref[pl.ds(a, c), :])).astype(y_ref.dtype)

    def body(i, carry, done):
        st_f, st_b = carry
        a = _aligned(i * c, c)
        b = _aligned((n - 1 - i) * c, c)
        (qf, kf), (qb, kb) = load_qk(a), load_qk(b)
        vf, vb = v_ref[pl.ds(a, c), :], v_ref[pl.ds(b, c), :]
        sc_f, sc_b = _mm_nt(qf, kf), _mm_nt(qb, kb)
        in_f, in_b = _mm(qf * eg_f, st_f), _mm(qb * eg_b, st_b)
        up_f, up_b = _mm_tn(kf * ed_f, vf), _mm_tn(kb * ed_b, vb)
        of_scr[pl.ds(a, c), :] = _mm(sc_f * dmat_f, vf) + in_f
        ob_scr[pl.ds(b, c), :] = _mm(sc_b * dmat_b, vb) + in_b
        _finish_visited([(a, 0), (b, 1)], done, finish)
        return st_f * tot_f + up_f, st_b * tot_b + up_b

    st_f, st_b = _run_trips(n, body, (s0f_ref[0, 0], s0b_ref[0, 0]))
    sf_ref[0, 0] = st_f
    sb_ref[0, 0] = st_b


def _ret(p, b, t, cos_t, sin_t, ng, nb, s0f, s0b, on_grid):
    c = min(CHUNK_C, t)
    q_off, k_off = 0, WC_QK // K_C
    v_off, z_off = (2 * WC_QK) // V_C, (2 * WC_QK + WC_V) // V_C
    st_shape = jax.ShapeDtypeStruct((b, H_C, K_C, V_C), F32)
    st_spec = pl.BlockSpec((1, 1, K_C, V_C), lambda i, h: (i, h, 0, 0))
    tab_spec = pl.BlockSpec((t, K_C), lambda i, h: (0, 0))
    vrow = pl.BlockSpec((1, V_C), lambda i, h: (0, 0))
    return pl.pallas_call(
        functools.partial(_ret_kernel, t=t, c=c, on_grid=on_grid),
        out_shape=(jax.ShapeDtypeStruct((b * t, WC_V), MXU_DTYPE), st_shape, st_shape),
        grid=(b, H_C),
        in_specs=[pl.BlockSpec((t, K_C), lambda i, h: (i, q_off + h)),
                  pl.BlockSpec((t, K_C), lambda i, h: (i, k_off + h)),
                  pl.BlockSpec((t, V_C), lambda i, h: (i, v_off + h)),
                  pl.BlockSpec((t, V_C), lambda i, h: (i, z_off + h)),
                  tab_spec, tab_spec, vrow, vrow, st_spec, st_spec],
        out_specs=(pl.BlockSpec((t, V_C), lambda i, h: (i, h)), st_spec, st_spec),
        scratch_shapes=[pltpu.VMEM((t, V_C), F32), pltpu.VMEM((t, V_C), F32)],
        compiler_params=_cparams(("arbitrary", "arbitrary")),
        name="ret_scan",
    )(p, p, p, p, cos_t, sin_t, ng, nb, s0f, s0b)


def _ssd_chunks(chains, states, consts, expands, c):
    hp = HEADS_PER_GROUP
    w = hp * P_D
    gates = [_sel2_r(ch["dt_la"], expands[ch["rev"]]) for ch in chains]
    for ch, gt in zip(chains, gates):
        ch["v"], ch["la_x"] = ch["x"] * gt[:, :w], gt[:, w:]
    cums = [_sel2_l(consts[ch["rev"]][0], jnp.concatenate([ch["la_x"] * consts[ch["rev"]][1], ch["la_x"]], axis=1))
            for ch in chains]
    scores = [_mm_nt(ch["cm"], jnp.concatenate([ch["bd"]] * hp, axis=0)) for ch in chains]
    for ch, x, sc in zip(chains, cums, scores):
        gx = x[:, w:]
        ch["gx"], ch["gl"] = gx, (gx[0:1] if ch["rev"] else gx[c - 1:c])
        ch["p"] = sc * jnp.exp(x[:, :w]) * consts[ch["rev"]][2]
    intra = [_mm(ch["p"], jnp.concatenate([ch["v"]] * hp, axis=0) * consts[ch["rev"]][3]) for ch in chains]
    upd = [_mm_tn(ch["bd"], ch["v"] * jnp.exp(ch["gl"] - ch["gx"])) for ch in chains]
    outs, states = [], dict(states)
    for ch, oi, up in zip(chains, intra, upd):
        st = states[ch["rev"]]
        outs.append(oi + _mm(ch["cm"], st) * jnp.exp(ch["gx"]))
        states[ch["rev"]] = st * jnp.exp(ch["gl"]) + up
    return outs, states


def _ssd_kernel(x_ref, b_ref, c_ref, z_ref, sm_ref, wx_ref, wb_ref, wc_ref, bx_ref, bb_ref, bc_ref,
                dsk_ref, ng_ref, s0f_ref, s0b_ref,
                y_ref, sf_ref, sb_ref, xs, bs, cs, of_scr, ob_scr, *, t, c):
    n = t // c
    hp = HEADS_PER_GROUP
    w = hp * P_D
    grp = pl.program_id(1)
    erow, ecol = _iota2((LANE, 2 * w), 0), _iota2((LANE, 2 * w), 1)
    head = grp * hp + (ecol % w) // P_D + jnp.where(ecol < w, 0, CD_LA)
    expand_f = _f01(erow == CD_DTF + head)
    expand_b = _f01(erow == CD_DTB + head)

    def prep(i, _):
        a = pl.multiple_of(i * c, c)
        xs[pl.ds(a, c), :] = _silu(_conv3_chunk(x_ref, wx_ref, a, c, t, bx_ref[...]))
        bs[pl.ds(a, c), :] = _silu(_conv3_chunk(b_ref, wb_ref, a, c, t, bb_ref[...]))
        cs[pl.ds(a, c), :] = _silu(_conv3_chunk(c_ref, wc_ref, a, c, t, bc_ref[...]))
        return 0

    lax.fori_loop(0, n, prep, 0)

    row, col = _iota2((c, c), 0), _iota2((c, c), 1)
    rt, ct = _iota2((c, hp * c), 0), _iota2((c, hp * c), 1) % c
    brow, bcol = _iota2((hp * c, w), 0), _iota2((hp * c, w), 1)
    bd_mask = _f01((brow // c) == (bcol // P_D))
    consts = {False: (_f01(col <= row), _f01(rt > ct), _f01(ct <= rt), bd_mask),
              True: (_f01(col >= row), _f01(rt < ct), _f01(ct >= rt), bd_mask)}
    expands = {False: expand_f, True: expand_b}
    o_scrs = {False: of_scr, True: ob_scr}

    def finish(a):
        o = of_scr[pl.ds(a, c), :] + ob_scr[pl.ds(a, c), :] + dsk_ref[...] * xs[pl.ds(a, c), :]
        o = o * _silu(z_ref[pl.ds(a, c), :])
        y = o * lax.rsqrt(jnp.mean(o * o, axis=-1, keepdims=True) + 1e-6) * ng_ref[...]
        y_ref[pl.ds(a, c), :] = y.astype(y_ref.dtype)

    def body(i, carry, done):
        chains = []
        for cc in range(SSD_CHUNKS):
            for rev in (False, True):
                ci = i * SSD_CHUNKS + cc
                a = _aligned((n - 1 - ci if rev else ci) * c, c)
                chains.append(dict(a=a, rev=rev, cm=cs[pl.ds(a, c), :], bd=bs[pl.ds(a, c), :],
                                   x=xs[pl.ds(a, c), :], dt_la=sm_ref[pl.ds(a, c), :]))
        outs, states = _ssd_chunks(chains, {False: carry[0], True: carry[1]}, consts, expands, c)
        for ch, o in zip(chains, outs):
            o_scrs[ch["rev"]][pl.ds(ch["a"], c), :] = o
        _finish_visited([(ch["a"], int(ch["rev"])) for ch in chains], done, finish)
        return states[False], states[True]

    st_f, st_b = _run_trips(n // SSD_CHUNKS, body, (s0f_ref[0, 0], s0b_ref[0, 0]))
    sf_ref[0, 0] = st_f
    sb_ref[0, 0] = st_b


def _ssd(p, ps, b, t, conv_w, conv_b, dskip, ng, s0f, s0b):
    c = CHUNK_D
    w = HEADS_PER_GROUP * P_D
    base = 2 * WC_QK + 2 * WC_V
    x_off = base // w
    b_off = (base + WD) // N_D
    c_off = (base + WD + WD_BC) // N_D
    z_off = (base + WD + 2 * WD_BC) // w
    sm_blk = CD_MAIN // LANE
    st_shape = jax.ShapeDtypeStruct((b, G_D, N_D, w), F32)
    st_spec = pl.BlockSpec((1, 1, N_D, w), lambda i, g: (i, g, 0, 0))
    row_spec = pl.BlockSpec((1, LANE), lambda i, g: (0, 0))
    return pl.pallas_call(
        functools.partial(_ssd_kernel, t=t, c=c),
        out_shape=(jax.ShapeDtypeStruct((b * t, WD), MXU_DTYPE), st_shape, st_shape),
        grid=(b, G_D),
        in_specs=[pl.BlockSpec((t, w), lambda i, g: (i, x_off + g)),
                  pl.BlockSpec((t, N_D), lambda i, g: (i, b_off + g)),
                  pl.BlockSpec((t, N_D), lambda i, g: (i, c_off + g)),
                  pl.BlockSpec((t, w), lambda i, g: (i, z_off + g)),
                  pl.BlockSpec((t, LANE), lambda i, g: (i, 0)),
                  pl.BlockSpec((3, w), lambda i, g: (0, g)),
                  pl.BlockSpec((3, N_D), lambda i, g: (0, WD // N_D + g)),
                  pl.BlockSpec((3, N_D), lambda i, g: (0, (WD + WD_BC) // N_D + g)),
                  pl.BlockSpec((1, w), lambda i, g: (0, g)),
                  pl.BlockSpec((1, N_D), lambda i, g: (0, WD // N_D + g)),
                  pl.BlockSpec((1, N_D), lambda i, g: (0, (WD + WD_BC) // N_D + g)),
                  pl.BlockSpec((1, w), lambda i, g: (0, g)),
                  pl.BlockSpec((1, w), lambda i, g: (0, g)),
                  st_spec, st_spec],
        out_specs=(pl.BlockSpec((t, w), lambda i, g: (i, g)), st_spec, st_spec),
        scratch_shapes=[pltpu.VMEM((t, w), F32), pltpu.VMEM((t, N_D), F32), pltpu.VMEM((t, N_D), F32),
                        pltpu.VMEM((t, w), F32), pltpu.VMEM((t, w), F32)],
        compiler_params=_cparams(("arbitrary", "arbitrary")),
        name="ssd_scan",
    )(p, p, p, p, ps, conv_w, conv_w, conv_w, conv_b, conv_b, conv_b, dskip, ng, s0f, s0b)


def _split_cols(w, sizes):
    out, o = [], 0
    for s in sizes:
        out.append(w[..., o:o + s])
        o += s
    return out


def _pad_cols(w, n):
    return jnp.pad(w, [(0, 0)] * (w.ndim - 1) + [(0, n - w.shape[-1])])


def _lane_row(pieces):
    row = jnp.zeros((LANE,), F32)
    for off, vec in pieces:
        row = lax.dynamic_update_slice(row, vec.astype(F32), (off,))
    return row.reshape(1, LANE)


def _rotary_tables(t):
    n_rows = t // GRID_W
    rows = jnp.repeat(jnp.arange(n_rows), GRID_W).astype(F32)
    cols = jnp.tile(jnp.arange(GRID_W), n_rows).astype(F32)
    n_freq = K_C // 4
    inv_freq = ROPE_BASE ** (-jnp.arange(n_freq, dtype=F32) / n_freq)
    ang = jnp.concatenate([rows[:, None] * inv_freq, cols[:, None] * inv_freq], axis=-1)
    cos = jnp.repeat(jnp.cos(ang), 2, axis=-1)
    sin = jnp.repeat(jnp.sin(ang), 2, axis=-1)
    sign = jnp.tile(jnp.array([-1.0, 1.0], F32), K_C // 2)
    return cos, sin * sign


def kernel(x_prompt, x_sample, state_gla_fwd, state_gla_bwd, state_gdn_fwd, state_gdn_bwd, state_ret_fwd, state_ret_bwd, state_ssd_fwd, state_ssd_bwd, c, c_ctx, ada_w, ada_b, norm_g, final_norm_g, ab_w_in, ab_w_out, gla_gate_w2, gla_gate_b, gla_norm_g, gdn_conv_w, gdn_a_log, gdn_dt_bias, gdn_norm_g, cd_w_in, cd_w_out, ret_norm_g, ret_norm_b, ssd_conv_w, ssd_conv_b, ssd_a_log, ssd_dt_bias, ssd_d, ssd_norm_g):
    bp, tp, _ = x_prompt.shape
    bs, ts, _ = x_sample.shape
    hp = HEADS_PER_GROUP

    cvec = jnp.concatenate([c_ctx[None, :], c], axis=0)
    cvec16 = jnp.pad(cvec, ((0, 16 - cvec.shape[0]), (0, 0)))
    mod = _ada_mod(cvec16, ada_w, ada_b).reshape(DEPTH, 16, 3, D_MODEL)

    groups = [
        dict(x=x_prompt, b=bp, t=tp, shared=True, on_grid=False, rows=slice(0, 1)),
        dict(x=x_sample, b=bs, t=ts, shared=False, on_grid=True, rows=slice(1, 1 + bs)),
    ]
    cos_t, sin_t = _rotary_tables(ts)
    xs = [g["x"].reshape(g["b"] * g["t"], D_MODEL) for g in groups]
    new_states = {}

    for l in range(DEPTH):
        i = l // 2
        if l % 2 == 0:
            w = ab_w_in[i]
            w_in, w_small = _regroup_ab_weight(w)
            w_out = ab_w_out[i].astype(MXU_DTYPE)
            wg = jnp.zeros((2, LANE, WA_QK), F32)
            wg = wg.at[0, AB_RF:AB_RF + GLA_RANK].set(gla_gate_w2[i, 0])
            wg = wg.at[1, AB_RB:AB_RB + GLA_RANK].set(gla_gate_w2[i, 1])
            bg = gla_gate_b[i].reshape(2, 1, WA_QK)
            dtb_row = _lane_row([(AB_AF, gdn_dt_bias[i, 0]), (AB_AB, gdn_dt_bias[i, 1])])
            alog_row = _lane_row([(AB_AF, gdn_a_log[i, 0]), (AB_AB, gdn_a_log[i, 1])])
        else:
            w = cd_w_in[i]
            w_in = w.astype(MXU_DTYPE)
            w_small = jnp.concatenate([w[:, CD_MAIN:].astype(MXU_DTYPE),
                                       jnp.zeros((D_MODEL, LANE - 2 * H_D), MXU_DTYPE)], axis=1)
            w_out = cd_w_out[i].astype(MXU_DTYPE)
            dtb_row = _lane_row([(CD_DTF, ssd_dt_bias[i, 0]), (CD_DTB, ssd_dt_bias[i, 1])])
            alog_row = _lane_row([(CD_DTF, ssd_a_log[i, 0]), (CD_DTB, ssd_a_log[i, 1])])
            dskip = jnp.repeat(ssd_d[i], P_D).reshape(1, WD)

        for gi, g in enumerate(groups):
            b, t = g["b"], g["t"]
            mod_g = mod[l, g["rows"]]
            h = _norm_mod(xs[gi].reshape(b, t, D_MODEL), norm_g[l], mod_g, g["shared"])
            p, ps = _proj(h, w_in, w_small, dtb_row, alog_row, "ab" if l % 2 == 0 else "cd")
            if l % 2 == 0:
                if gi == 0:
                    sa_f = sa_b = jnp.zeros((b, H_A, V_A, K_A), F32)
                    sb_f = sb_b = jnp.zeros((b, H_B, K_B, V_B), F32)
                else:
                    sa_f = jnp.swapaxes(state_gla_fwd[:, i], -1, -2)
                    sa_b = jnp.swapaxes(state_gla_bwd[:, i], -1, -2)
                    sb_f, sb_b = state_gdn_fwd[:, i], state_gdn_bwd[:, i]
                y1, fa_f, fa_b = _gla(p, ps, b, t, wg, bg, gla_norm_g[i].reshape(1, V_A), sa_f, sa_b)
                y2, fb_f, fb_b = _gdn(p, ps, b, t, gdn_conv_w[i], gdn_norm_g[i].reshape(1, V_B), sb_f, sb_b)
                if gi == 0:
                    new_states["gla_f"] = jnp.swapaxes(fa_f, -1, -2)
                    new_states["gla_b"] = jnp.swapaxes(fa_b, -1, -2)
                    new_states["gdn_f"], new_states["gdn_b"] = fb_f, fb_b
            else:
                if gi == 0:
                    sc_f = sc_b = jnp.zeros((b, H_C, K_C, V_C), F32)
                    sd_f = sd_b = jnp.zeros((b, G_D, N_D, hp * P_D), F32)
                else:
                    sc_f, sc_b = state_ret_fwd[:, i], state_ret_bwd[:, i]

                    def to_grp(s):
                        s = s.reshape(b, G_D, hp, N_D, P_D)
                        return jnp.transpose(s, (0, 1, 3, 2, 4)).reshape(b, G_D, N_D, hp * P_D)

                    sd_f, sd_b = to_grp(state_ssd_fwd[:, i]), to_grp(state_ssd_bwd[:, i])
                tab_c, tab_s = (cos_t, sin_t) if g["on_grid"] else (cos_t[:t], sin_t[:t])
                y1, fc_f, fc_b = _ret(p, b, t, tab_c, tab_s, ret_norm_g[i].reshape(1, V_C),
                                      ret_norm_b[i].reshape(1, V_C), sc_f, sc_b, g["on_grid"])
                y2, fd_f, fd_b = _ssd(p, ps, b, t, ssd_conv_w[i], ssd_conv_b[i].reshape(1, -1),
                                      dskip, ssd_norm_g[i].reshape(1, WD), sd_f, sd_b)
                if gi == 0:
                    def from_grp(s):
                        s = s.reshape(b, G_D, N_D, hp, P_D)
                        return jnp.transpose(s, (0, 1, 3, 2, 4)).reshape(b, H_D, N_D, P_D)

                    new_states["ret_f"], new_states["ret_b"] = fc_f, fc_b
                    new_states["ssd_f"], new_states["ssd_b"] = from_grp(fd_f), from_grp(fd_b)
            xs[gi] = _out_proj(y1, y2, w_out, xs[gi], mod_g, t, g["shared"])

    y_prompt = _final_norm(xs[0], final_norm_g).reshape(bp, tp, D_MODEL)
    y_sample = _final_norm(xs[1], final_norm_g).reshape(bs, ts, D_MODEL)
    st = lambda k: new_states[k][:, None]
    return (y_prompt, y_sample, st("gla_f"), st("gla_b"), st("gdn_f"), st("gdn_b"),
            st("ret_f"), st("ret_b"), st("ssd_f"), st("ssd_b"))
```

```python
import functools
import math

import jax
import jax.numpy as jnp
from jax import lax
from jax.experimental import pallas as pl
from jax.experimental.pallas import tpu as pltpu

F32 = jnp.float32
MXU_DTYPE = jnp.bfloat16

D_MODEL = 4096
DEPTH = 2
GRID_W = 64
ROPE_BASE = 10000.0
H_A, K_A, V_A = 8, 128, 256
GLA_RANK = 16
GLA_GATE_NORM = 16.0
H_B, K_B, V_B = 16, 128, 128
H_C, K_C, V_C = 8, 128, 256
H_D, N_D, P_D, G_D = 32, 128, 64, 4
HEADS_PER_GROUP = H_D // G_D
WA_QK, WA_V = H_A * K_A, H_A * V_A
WB = H_B * K_B
WC_QK, WC_V = H_C * K_C, H_C * V_C
WD, WD_BC = H_D * P_D, G_D * N_D
AB_SIZES = (WA_QK, WA_QK, WA_V, WA_V, GLA_RANK, GLA_RANK, 3 * WB, WB, H_B, H_B, H_B, H_B)
CD_SIZES = (WC_QK, WC_QK, WC_V, WC_V, WD + 2 * WD_BC, WD, H_D, H_D)

LANE = 128
VMEM_LIMIT = 52 * 1024 * 1024

AB_MAIN = 2 * WA_QK + 2 * WA_V + 4 * WB
CD_MAIN = 2 * WC_QK + 2 * WC_V + (WD + 2 * WD_BC) + WD
TN_PROJ = 1024
AB_RF, AB_RB, AB_BF, AB_BB, AB_AF, AB_AB = 0, 16, 32, 48, 64, 80
CD_DTF, CD_DTB, CD_LA = 0, 32, 64

CHUNK_A = 64
SUB_A = 16
HEADS_A = 2
GLA_CHUNKS = 4
CHUNK_B = 128
HEADS_B = 2
PREP_CHUNKS_B = 4
INV_BASE = 8
CHUNK_C = 256
CHUNK_D = P_D
SSD_CHUNKS = 4


def _mx(x):
    return x.astype(MXU_DTYPE)


def _mm(a, b):
    return jnp.dot(_mx(a), _mx(b), preferred_element_type=F32)


def _mm_nt(a, b):
    return lax.dot_general(_mx(a), _mx(b), (((1,), (1,)), ((), ())), preferred_element_type=F32)


def _mm_tn(a, b):
    return lax.dot_general(_mx(a), _mx(b), (((0,), (0,)), ((), ())), preferred_element_type=F32)


def _split3(x):
    hi = _mx(x)
    r1 = x - hi.astype(F32)
    mid = _mx(r1)
    lo = _mx(r1 - mid.astype(F32))
    return hi, mid, lo


def _sel_l(m01, x):
    hi, mid, lo = _split3(x)
    m = _mx(m01)
    d = functools.partial(jnp.dot, preferred_element_type=F32)
    return d(m, hi) + d(m, mid) + d(m, lo)


def _sel_r(x, m01):
    hi, mid, lo = _split3(x)
    m = _mx(m01)
    d = functools.partial(jnp.dot, preferred_element_type=F32)
    return d(hi, m) + d(mid, m) + d(lo, m)


def _sel2_l(m01, x):
    hi = _mx(x)
    lo = _mx(x - hi.astype(F32))
    m = _mx(m01)
    return jnp.dot(m, hi, preferred_element_type=F32) + jnp.dot(m, lo, preferred_element_type=F32)


def _sel2_r(x, m01):
    hi = _mx(x)
    lo = _mx(x - hi.astype(F32))
    m = _mx(m01)
    return jnp.dot(hi, m, preferred_element_type=F32) + jnp.dot(lo, m, preferred_element_type=F32)


def _mm3(a, b):
    a_hi = _mx(a)
    a_lo = _mx(a - a_hi.astype(F32))
    b_hi = _mx(b)
    b_lo = _mx(b - b_hi.astype(F32))
    d = functools.partial(jnp.dot, preferred_element_type=F32)
    return d(a_hi, b_hi) + d(a_hi, b_lo) + d(a_lo, b_hi)


def _silu(x):
    return x * (1.0 / (1.0 + jnp.exp(-x)))


def _sigmoid(x):
    return 1.0 / (1.0 + jnp.exp(-x))


def _softplus(x):
    return jnp.maximum(x, 0.0) + jnp.log1p(jnp.exp(-jnp.abs(x)))


def _log_sigmoid(x):
    return jnp.minimum(x, 0.0) - jnp.log1p(jnp.exp(-jnp.abs(x)))


def _iota2(shape, dim):
    return lax.broadcasted_iota(jnp.int32, shape, dim)


def _f01(mask):
    return jnp.where(mask, 1.0, 0.0).astype(F32)


def _shift_rows(x, prev_row, next_row):
    c = x.shape[0]
    r = _iota2(x.shape, 0)
    xp = jnp.where(r == 0, prev_row, pltpu.roll(x, 1, 0))
    xn = jnp.where(r == c - 1, next_row, pltpu.roll(x, c - 1, 0))
    return xp, xn


def _conv3_chunk(ref, w_ref, start, c, t_total, bias=None, lanes=slice(None)):
    x = ref[pl.ds(start, c), lanes]
    if isinstance(start, int):
        lo, hi = max(start - 8, 0), min(start + c, t_total - 8)
        has_prev, has_next = float(start > 0), float(start + c < t_total)
    else:
        lo = pl.multiple_of(jnp.maximum(start - 8, 0), 8)
        hi = pl.multiple_of(jnp.minimum(start + c, t_total - 8), 8)
        has_prev, has_next = (start > 0).astype(F32), (start + c < t_total).astype(F32)
    prev = ref[pl.ds(lo, 8), lanes][7:8] * has_prev
    nxt = ref[pl.ds(hi, 8), lanes][0:1] * has_next
    xp, xn = _shift_rows(x, prev, nxt)
    y = xp * w_ref[0:1, lanes] + x * w_ref[1:2, lanes] + xn * w_ref[2:3, lanes]
    if bias is not None:
        y = y + bias
    return y


def _aligned(x, c):
    return x if isinstance(x, int) else pl.multiple_of(x, c)


def _run_trips(m, body, init):
    if m == 1:
        return body(0, init, "same")
    assert m % 2 == 0
    carry = lax.fori_loop(0, m // 2, lambda i, cr: body(i, cr, None), init)
    return lax.fori_loop(m // 2, m, lambda i, cr: body(i, cr, "both"), carry)


def _finish_visited(starts, done, finish):
    for a, d in starts:
        if done == "both" or (done == "same" and d == 0):
            finish(a)


def _cparams(sem):
    return pltpu.CompilerParams(dimension_semantics=sem, vmem_limit_bytes=VMEM_LIMIT)


def _ada_kernel(c_ref, w_ref, b_ref, o_ref):
    a = _silu(c_ref[...])
    o_ref[0] = _mm3(a, w_ref[0]) + b_ref[0]


def _ada_mod(cvec16, ada_w, ada_b):
    tn = 512
    n3 = ada_w.shape[-1]
    return pl.pallas_call(
        _ada_kernel,
        out_shape=jax.ShapeDtypeStruct((DEPTH, 16, n3), F32),
        grid=(DEPTH, n3 // tn),
        in_specs=[pl.BlockSpec((16, D_MODEL), lambda l, j: (0, 0)),
                  pl.BlockSpec((1, D_MODEL, tn), lambda l, j: (l, 0, j)),
                  pl.BlockSpec((1, 1, tn), lambda l, j: (l, 0, j))],
        out_specs=pl.BlockSpec((1, 16, tn), lambda l, j: (l, 0, j)),
        compiler_params=_cparams(("arbitrary", "arbitrary")),
        name="ada_mod",
    )(cvec16, ada_w, ada_b.reshape(DEPTH, 1, n3))


def _norm_mod_kernel(x_ref, g_ref, m_ref, o_ref):
    x = x_ref[0]
    y = x * lax.rsqrt(jnp.mean(x * x, axis=-1, keepdims=True) + 1e-6)
    y = y * g_ref[...]
    o_ref[...] = (y * (1.0 + m_ref[0, 1:2, :]) + m_ref[0, 0:1, :]).astype(o_ref.dtype)


def _norm_mod(x, g, mod, shared):
    b, t, d = x.shape
    tt = 256
    nt = t // tt
    mmap = (lambda i, j: (0, 0, 0)) if shared else (lambda i, j: (i, 0, 0))
    return pl.pallas_call(
        _norm_mod_kernel,
        out_shape=jax.ShapeDtypeStruct((b * t, d), MXU_DTYPE),
        grid=(b, nt),
        in_specs=[pl.BlockSpec((1, tt, d), lambda i, j: (i, j, 0)),
                  pl.BlockSpec((1, d), lambda i, j: (0, 0)),
                  pl.BlockSpec((1, 3, d), mmap)],
        out_specs=pl.BlockSpec((tt, d), lambda i, j: (i * nt + j, 0)),
        compiler_params=_cparams(("arbitrary", "arbitrary")),
        name="norm_mod",
    )(x, g.reshape(1, d), mod)


def _regroup_kernel(a_ref, b_ref, o_ref, os_ref, *, first_shifted, last, shift, tail):
    j = pl.program_id(1)

    @pl.when(j < first_shifted)
    def _():
        o_ref[...] = a_ref[...].astype(o_ref.dtype)

    @pl.when(j >= first_shifted)
    def _():
        o_ref[...] = jnp.concatenate([a_ref[:, shift:], b_ref[:, :shift]], axis=1).astype(o_ref.dtype)

    @pl.when(j == first_shifted - 1)
    def _():
        os_ref[:, :shift] = b_ref[:, :shift].astype(os_ref.dtype)

    @pl.when(j == last)
    def _():
        os_ref[:, shift:shift + tail] = b_ref[:, shift:shift + tail].astype(os_ref.dtype)
        os_ref[:, shift + tail:] = jnp.zeros((os_ref.shape[0], LANE - shift - tail), os_ref.dtype)


def _regroup_ab_weight(w):
    d = w.shape[0]
    n_a = 2 * WA_QK + 2 * WA_V
    shift, tail = 2 * GLA_RANK, 4 * H_B
    tr, tc = 1024, 1024
    assert n_a % tc == 0 and AB_MAIN % tc == 0 and shift + tail <= LANE
    assert w.shape[1] == AB_MAIN + shift + tail
    nj = AB_MAIN // tc
    return pl.pallas_call(
        functools.partial(_regroup_kernel, first_shifted=n_a // tc, last=nj - 1, shift=shift, tail=tail),
        out_shape=(jax.ShapeDtypeStruct((d, AB_MAIN), MXU_DTYPE), jax.ShapeDtypeStruct((d, LANE), MXU_DTYPE)),
        grid=(d // tr, nj),
        in_specs=[pl.BlockSpec((tr, tc), lambda r, j: (r, j)),
                  pl.BlockSpec((tr, LANE), lambda r, j: (r, (j + 1) * (tc // LANE)))],
        out_specs=(pl.BlockSpec((tr, tc), lambda r, j: (r, j)),
                   pl.BlockSpec((tr, LANE), lambda r, j: (r, 0))),
        compiler_params=_cparams(("arbitrary", "arbitrary")),
        name="regroup_w",
    )(w, w)


def _gate_lanes(x, dtb, alog, kind):
    lane = _iota2((1, LANE), 1)
    neg_a = -jnp.exp(alog)
    if kind == "ab":
        return jnp.where(lane < AB_BF, x, jnp.where(lane < AB_AF, _sigmoid(x), neg_a * _softplus(x + dtb)))
    dt = _softplus(x + dtb)
    return jnp.where(lane < CD_LA, dt, pltpu.roll(neg_a * dt, CD_LA, 1))


def _proj_kernel(h_ref, w_ref, ws_ref, dtb_ref, alog_ref, o_ref, os_ref, *, kind):
    o_ref[...] = jnp.dot(h_ref[...], w_ref[...], preferred_element_type=F32)

    @pl.when(pl.program_id(1) == 0)
    def _():
        x = jnp.dot(h_ref[...], ws_ref[...], preferred_element_type=F32)
        os_ref[...] = _gate_lanes(x, dtb_ref[...], alog_ref[...], kind)


def _proj(h, w, w_small, dtb_row, alog_row, kind):
    m, d = h.shape
    n = AB_MAIN if kind == "ab" else CD_MAIN
    assert w.shape[1] >= n and n % TN_PROJ == 0
    tm = min(1024, m)
    row_spec = pl.BlockSpec((1, LANE), lambda i, j: (0, 0))
    return pl.pallas_call(
        functools.partial(_proj_kernel, kind=kind),
        out_shape=(jax.ShapeDtypeStruct((m, n), F32), jax.ShapeDtypeStruct((m, LANE), F32)),
        grid=(m // tm, n // TN_PROJ),
        in_specs=[pl.BlockSpec((tm, d), lambda i, j: (i, 0)),
                  pl.BlockSpec((d, TN_PROJ), lambda i, j: (0, j)),
                  pl.BlockSpec((d, LANE), lambda i, j: (0, 0)),
                  row_spec, row_spec],
        out_specs=(pl.BlockSpec((tm, TN_PROJ), lambda i, j: (i, j)),
                   pl.BlockSpec((tm, LANE), lambda i, j: (i, 0))),
        compiler_params=_cparams(("arbitrary", "arbitrary")),
        name="in_proj",
    )(h, w, w_small, dtb_row, alog_row)


def _out_kernel(y1_ref, y2_ref, w1_ref, w2_ref, x_ref, m_ref, o_ref):
    acc = jnp.dot(y1_ref[...], w1_ref[...], preferred_element_type=F32)
    acc = acc + jnp.dot(y2_ref[...], w2_ref[...], preferred_element_type=F32)
    o_ref[...] = x_ref[...] + m_ref[0, 2:3, :] * acc


def _out_proj(y1, y2, w, x2d, mod, t, shared):
    m, d = x2d.shape
    k1, k2 = y1.shape[1], y2.shape[1]
    assert k1 == k2 and w.shape[0] == k1 + k2
    tm, tn = 1024, 512
    per_b = t // tm if t >= tm else 1
    mmap = (lambda i, j: (0, 0, j)) if shared else (lambda i, j: (i // per_b, 0, j))
    return pl.pallas_call(
        _out_kernel,
        out_shape=jax.ShapeDtypeStruct((m, d), F32),
        grid=(m // tm, d // tn),
        in_specs=[pl.BlockSpec((tm, k1), lambda i, j: (i, 0)),
                  pl.BlockSpec((tm, k2), lambda i, j: (i, 0)),
                  pl.BlockSpec((k1, tn), lambda i, j: (0, j)),
                  pl.BlockSpec((k2, tn), lambda i, j: (1, j)),
                  pl.BlockSpec((tm, tn), lambda i, j: (i, j)),
                  pl.BlockSpec((1, 3, tn), mmap)],
        out_specs=pl.BlockSpec((tm, tn), lambda i, j: (i, j)),
        compiler_params=_cparams(("arbitrary", "arbitrary")),
        name="out_proj",
    )(y1, y2, w, w, x2d, mod)


def _final_norm_kernel(x_ref, g_ref, o_ref):
    x = x_ref[...]
    o_ref[...] = x * lax.rsqrt(jnp.mean(x * x, axis=-1, keepdims=True) + 1e-6) * g_ref[...]


def _final_norm(x2d, g):
    m, d = x2d.shape
    tt = 256
    return pl.pallas_call(
        _final_norm_kernel,
        out_shape=jax.ShapeDtypeStruct((m, d), F32),
        grid=(m // tt,),
        in_specs=[pl.BlockSpec((tt, d), lambda i: (i, 0)),
                  pl.BlockSpec((1, d), lambda i: (0, 0))],
        out_specs=pl.BlockSpec((tt, d), lambda i: (i, 0)),
        compiler_params=_cparams(("arbitrary",)),
        name="final_norm",
    )(x2d, g.reshape(1, d))


def _gla_consts(c, s):
    row, col = _iota2((c, c), 0), _iota2((c, c), 1)
    same = (col // s) == (row // s)
    out = {}
    for rev in (False, True):
        tri = (col >= row) if rev else (col <= row)
        out[rev] = (jnp.concatenate([_f01(tri), _f01(jnp.logical_and(tri, same))], axis=0), _f01(tri))
    return out


def _gla_chunks(chains, states, consts, c, s):
    nb = c // s
    rowi = _iota2((c, LANE), 0)
    cums = [_sel2_l(consts[ch["rev"]][0], ch["l"]) for ch in chains]
    score_blocks = []
    for ch, x in zip(chains, cums):
        g, cb = x[:c], x[c:]
        ch["g"] = g
        ch["gl"] = g[0:1] if ch["rev"] else g[c - 1:c]
        qt = ch["q"] * jnp.exp(cb)
        blocks = []
        for i in range(nb):
            if ch["rev"]:
                g_ref = g[(i + 1) * s:(i + 1) * s + 1] if i < nb - 1 else jnp.zeros((1, LANE), F32)
                valid = rowi >= i * s
            else:
                g_ref = g[i * s - 1:i * s] if i > 0 else jnp.zeros((1, LANE), F32)
                valid = rowi < (i + 1) * s
            kh = ch["k"] * jnp.exp(jnp.where(valid, g_ref - g, -1e30))
            blocks.append(_mm_nt(qt[i * s:(i + 1) * s], kh))
        score_blocks.append(blocks)
    upd = [_mm_tn(ch["v"], ch["k"] * jnp.exp(ch["gl"] - ch["g"])) for ch in chains]
    intra = [_mm(jnp.where(consts[ch["rev"]][1] > 0.5, jnp.concatenate(blocks, axis=0), 0.0), ch["v"])
             for ch, blocks in zip(chains, score_blocks)]
    outs, states = [], list(states)
    for ch, oi, up in zip(chains, intra, upd):
        st = states[ch["slot"]]
        outs.append(oi + _mm_nt(ch["q"] * jnp.exp(ch["g"]), st))
        states[ch["slot"]] = st * jnp.exp(ch["gl"]) + up
    return outs, states


def _gla_kernel(q_ref, k_ref, v_ref, z_ref, sm_ref, wg_ref, bg_ref, ng_ref, s0f_ref, s0b_ref,
                y_ref, sf_ref, sb_ref, o_scr, *, t, c, s, hb):
    n = t // c
    consts = _gla_consts(c, s)

    def finish(a):
        for j in range(hb):
            vl = slice(j * V_A, (j + 1) * V_A)
            o = o_scr[2 * j, pl.ds(a, c), :] + o_scr[2 * j + 1, pl.ds(a, c), :]
            y = o * lax.rsqrt(jnp.mean(o * o, axis=-1, keepdims=True) + 1e-6) * ng_ref[...]
            y_ref[pl.ds(a, c), vl] = (y * _silu(z_ref[pl.ds(a, c), vl])).astype(y_ref.dtype)

    def body(i, carry, done):
        chains, starts = [], []
        for cc in range(GLA_CHUNKS):
            ci = i * GLA_CHUNKS + cc
            for d in range(2):
                starts.append((_aligned((ci if d == 0 else n - 1 - ci) * c, c), d))
            for j in range(hb):
                kl, vl = slice(j * K_A, (j + 1) * K_A), slice(j * V_A, (j + 1) * V_A)
                for a, d in starts[-2:]:
                    logit = _mm(sm_ref[pl.ds(a, c), :], wg_ref[d, :, kl]) + bg_ref[d, :, kl]
                    chains.append(dict(a=a, rev=d == 1, slot=2 * j + d,
                                       l=_log_sigmoid(logit) * (1.0 / GLA_GATE_NORM),
                                       q=q_ref[pl.ds(a, c), kl] * (K_A ** -0.5), k=k_ref[pl.ds(a, c), kl],
                                       v=v_ref[pl.ds(a, c), vl]))
        outs, states = _gla_chunks(chains, carry, consts, c, s)
        for ch, o in zip(chains, outs):
            o_scr[ch["slot"], pl.ds(ch["a"], c), :] = o
        _finish_visited(starts, done, finish)
        return tuple(states)

    init = []
    for j in range(hb):
        init += [s0f_ref[0, j], s0b_ref[0, j]]
    fin = _run_trips(n // GLA_CHUNKS, body, tuple(init))
    for j in range(hb):
        sf_ref[0, j] = fin[2 * j]
        sb_ref[0, j] = fin[2 * j + 1]


def _gla(p, ps, b, t, wg, bg, ng, s0f, s0b):
    c, s, hb = CHUNK_A, SUB_A, HEADS_A
    ngrp = H_A // hb
    wk, wv = hb * K_A, hb * V_A
    q_off, k_off = 0, WA_QK // wk
    v_off, z_off = (2 * WA_QK) // wv, (2 * WA_QK + WA_V) // wv
    sm_blk = AB_MAIN // LANE
    st_shape = jax.ShapeDtypeStruct((b, H_A, V_A, K_A), F32)
    st_spec = pl.BlockSpec((1, hb, V_A, K_A), lambda i, h: (i, h, 0, 0))
    return pl.pallas_call(
        functools.partial(_gla_kernel, t=t, c=c, s=s, hb=hb),
        out_shape=(jax.ShapeDtypeStruct((b * t, WA_V), MXU_DTYPE), st_shape, st_shape),
        grid=(b, ngrp),
        in_specs=[pl.BlockSpec((t, wk), lambda i, h: (i, q_off + h)),
                  pl.BlockSpec((t, wk), lambda i, h: (i, k_off + h)),
                  pl.BlockSpec((t, wv), lambda i, h: (i, v_off + h)),
                  pl.BlockSpec((t, wv), lambda i, h: (i, z_off + h)),
                  pl.BlockSpec((t, LANE), lambda i, h: (i, 0)),
                  pl.BlockSpec((2, LANE, wk), lambda i, h: (0, 0, h)),
                  pl.BlockSpec((2, 1, wk), lambda i, h: (0, 0, h)),
                  pl.BlockSpec((1, V_A), lambda i, h: (0, 0)),
                  st_spec, st_spec],
        out_specs=(pl.BlockSpec((t, wv), lambda i, h: (i, h)), st_spec, st_spec),
        scratch_shapes=[pltpu.VMEM((2 * hb, t, V_A), F32)],
        compiler_params=_cparams(("arbitrary", "arbitrary")),
        name="gla_scan",
    )(p, p, p, p, ps, wg, bg, ng, s0f, s0b)


def _tri_inverse_minus_eye(mats, c, masks):
    nm = [-a * masks["diag"] for a in mats]
    p2 = [_mm(n, n) for n in nm]
    y = [n + p + _mm(n, p) for n, p in zip(nm, p2)]
    p4 = [_mm(p, p) for p in p2]
    y = [yy + p + _mm(yy, p) for yy, p in zip(y, p4)]
    for coupled in masks["levels"]:
        lm = [a * coupled for a in mats]
        m = [l + _mm(l, yy) for l, yy in zip(lm, y)]
        y = [yy - (mm + _mm(yy, mm)) for yy, mm in zip(y, m)]
    return y


def _gdn_masks(c):
    row, col = _iota2((c, c), 0), _iota2((c, c), 1)
    levels, b = [], INV_BASE
    while b < c:
        levels.append(_f01(jnp.logical_and((row // (2 * b)) == (col // (2 * b)), (row // b) != (col // b))))
        b *= 2
    tri = {False: (_f01(col <= row), _f01(row > col), _f01(col <= row), _f01(col < row)),
           True: (_f01(col >= row), _f01(row < col), _f01(col >= row), _f01(col > row))}
    return dict(diag=_f01((row // INV_BASE) == (col // INV_BASE)), levels=levels, tri=tri)


def _gdn_prep(chains, c, masks):
    tri = masks["tri"]
    dg = []
    for ch in chains:
        m, bm = _mx(tri[ch["rev"]][0]), _mx(tri[ch["rev"]][1])
        hi = _mx(ch["la"])
        lo = _mx(ch["la"] - hi.astype(F32))
        dg.append(jnp.dot(m, jnp.concatenate([hi * bm, hi], axis=1), preferred_element_type=F32)
                  + jnp.dot(m, jnp.concatenate([lo * bm, lo], axis=1), preferred_element_type=F32))
    mats = []
    for ch, x in zip(chains, dg):
        g = x[:, c:]
        ch["g"], ch["eg"] = g, jnp.exp(g)
        ch["gl"] = g[0:1] if ch["rev"] else g[c - 1:c]
        ch["dec"] = jnp.exp(x[:, :c]) * tri[ch["rev"]][2]
        mats.append(ch["kk"] * ch["beta"] * ch["dec"] * tri[ch["rev"]][3])
    ys = _tri_inverse_minus_eye(mats, c, masks)
    rhs = [jnp.concatenate([ch["vc"] * ch["beta"], ch["kc"] * ch["beta"] * ch["eg"]], axis=1) for ch in chains]
    uw = [r + _mm(y, r) for r, y in zip(rhs, ys)]
    out = []
    for ch, x in zip(chains, uw):
        out.append(dict(u=x[:, :c], w=x[:, c:], attn=ch["qk"] * ch["dec"], qg=ch["qc"] * ch["eg"],
                        kd_t=(ch["kc"] * jnp.exp(ch["gl"] - ch["g"])).T, egl=jnp.exp(ch["gl"])))
    return out


def _l2n(x):
    return x * lax.rsqrt(jnp.sum(x * x, axis=-1, keepdims=True) + 1e-6)


def _gdn_kernel(q_ref, k_ref, v_ref, z_ref, sm_ref, wq_ref, wk_ref, wv_ref, ng_ref,
                s0f_ref, s0b_ref, y_ref, sf_ref, sb_ref,
                u_scr, wq_scr, at_scr, kd_scr, eg_scr, o_scr, *, t, c, hb):
    n = t // c
    grp = pl.program_id(1)
    srow, scol = _iota2((LANE, 4 * LANE), 0), _iota2((LANE, 4 * LANE), 1)
    base = jnp.where(scol < LANE, AB_BF, jnp.where(scol < 2 * LANE, AB_BB,
                                                   jnp.where(scol < 3 * LANE, AB_AF, AB_AB)))
    sels = [_f01(srow == base + (grp * hb + j)) for j in range(hb)]
    masks = _gdn_masks(c)

    pcs = min(PREP_CHUNKS_B, n)

    def prep(i, _):
        chains = []
        for cc in range(pcs):
            ci = i * pcs + cc
            a = pl.multiple_of(ci * c, c)
            gt = sm_ref[pl.ds(a, c), :]
            for j in range(hb):
                ln = slice(j * K_B, (j + 1) * K_B)
                qc = _l2n(_silu(_conv3_chunk(q_ref, wq_ref, a, c, t, lanes=ln))) * (K_B ** -0.5)
                kc = _l2n(_silu(_conv3_chunk(k_ref, wk_ref, a, c, t, lanes=ln)))
                vc = _silu(_conv3_chunk(v_ref, wv_ref, a, c, t, lanes=ln))
                kq = _mm_nt(jnp.concatenate([kc, qc], axis=0), kc)
                g4 = _sel2_r(gt, sels[j])
                for d in range(2):
                    chains.append(dict(a=a, ci=ci, s=2 * j + d, rev=d == 1, kk=kq[:c], qk=kq[c:],
                                       kc=kc, qc=qc, vc=vc, beta=g4[:, d * LANE:(d + 1) * LANE],
                                       la=g4[:, (2 + d) * LANE:(3 + d) * LANE]))
        for ch, r in zip(chains, _gdn_prep(chains, c, masks)):
            s, a = ch["s"], ch["a"]
            u_scr[s, pl.ds(a, c), :] = r["u"]
            a2 = pl.multiple_of(2 * a, 2 * c)
            wq_scr[s, pl.ds(a2, c), :] = r["w"].astype(wq_scr.dtype)
            wq_scr[s, pl.ds(a2 + c, c), :] = r["qg"].astype(wq_scr.dtype)
            at_scr[s, pl.ds(a, c), :] = r["attn"].astype(at_scr.dtype)
            kd_scr[s, :, pl.ds(a, c)] = r["kd_t"].astype(kd_scr.dtype)
            eg_scr[s, pl.ds(pl.multiple_of(ch["ci"] * 8, 8), 8), :] = jnp.broadcast_to(r["egl"], (8, LANE))
        return 0

    lax.fori_loop(0, n // pcs, prep, 0)

    def finish(a):
        for j in range(hb):
            ln = slice(j * V_B, (j + 1) * V_B)
            o = o_scr[2 * j, pl.ds(a, c), :] + o_scr[2 * j + 1, pl.ds(a, c), :]
            y = o * lax.rsqrt(jnp.mean(o * o, axis=-1, keepdims=True) + 1e-6) * ng_ref[...]
            y_ref[pl.ds(a, c), ln] = (y * _silu(z_ref[pl.ds(a, c), ln])).astype(y_ref.dtype)

    def body(i, carry, done):
        cis = [i if s % 2 == 0 else n - 1 - i for s in range(2 * hb)]
        starts = [_aligned(ci * c, c) for ci in cis]
        ws = [jnp.dot(wq_scr[s, pl.ds(_aligned(2 * starts[s], 2 * c), 2 * c), :], _mx(carry[s]),
                      preferred_element_type=F32) for s in range(2 * hb)]
        v_new = [_mx(u_scr[s, pl.ds(starts[s], c), :] - ws[s][:c]) for s in range(2 * hb)]
        out = []
        for s in range(2 * hb):
            o_scr[s, pl.ds(starts[s], c), :] = ws[s][c:] + jnp.dot(
                at_scr[s, pl.ds(starts[s], c), :], v_new[s], preferred_element_type=F32)
        for s in range(2 * hb):
            egl = eg_scr[s, pl.ds(_aligned(cis[s] * 8, 8), 1), :]
            out.append(carry[s] * egl + jnp.dot(kd_scr[s, :, pl.ds(starts[s], c)], v_new[s],
                                                preferred_element_type=F32))
        _finish_visited([(starts[0], 0), (starts[1], 1)], done, finish)
        return tuple(out)

    init = []
    for j in range(hb):
        init += [s0f_ref[0, j], s0b_ref[0, j]]
    fin = _run_trips(n, body, tuple(init))
    for j in range(hb):
        sf_ref[0, j] = fin[2 * j]
        sb_ref[0, j] = fin[2 * j + 1]


def _gdn(p, ps, b, t, conv_w, ng, s0f, s0b):
    c, hb = CHUNK_B, HEADS_B
    wblk = hb * K_B
    base = (2 * WA_QK + 2 * WA_V) // wblk
    ng_ = H_B // hb
    q_off, k_off, v_off, z_off = base, base + ng_, base + 2 * ng_, base + 3 * ng_
    sm_blk = AB_MAIN // LANE
    st_shape = jax.ShapeDtypeStruct((b, H_B, K_B, V_B), F32)
    st_spec = pl.BlockSpec((1, hb, K_B, V_B), lambda i, h: (i, h, 0, 0))
    row_spec = pl.BlockSpec((1, LANE), lambda i, h: (0, 0))
    ns = 2 * hb
    return pl.pallas_call(
        functools.partial(_gdn_kernel, t=t, c=c, hb=hb),
        out_shape=(jax.ShapeDtypeStruct((b * t, WB), MXU_DTYPE), st_shape, st_shape),
        grid=(b, ng_),
        in_specs=[pl.BlockSpec((t, wblk), lambda i, h: (i, q_off + h)),
                  pl.BlockSpec((t, wblk), lambda i, h: (i, k_off + h)),
                  pl.BlockSpec((t, wblk), lambda i, h: (i, v_off + h)),
                  pl.BlockSpec((t, wblk), lambda i, h: (i, z_off + h)),
                  pl.BlockSpec((t, LANE), lambda i, h: (i, 0)),
                  pl.BlockSpec((3, wblk), lambda i, h: (0, h)),
                  pl.BlockSpec((3, wblk), lambda i, h: (0, ng_ + h)),
                  pl.BlockSpec((3, wblk), lambda i, h: (0, 2 * ng_ + h)),
                  row_spec, st_spec, st_spec],
        out_specs=(pl.BlockSpec((t, wblk), lambda i, h: (i, h)), st_spec, st_spec),
        scratch_shapes=[pltpu.VMEM((ns, t, V_B), F32),
                        pltpu.VMEM((ns, 2 * t, K_B), MXU_DTYPE),
                        pltpu.VMEM((ns, t, c), MXU_DTYPE),
                        pltpu.VMEM((ns, K_B, t), MXU_DTYPE),
                        pltpu.VMEM((ns, (t // c) * 8, LANE), F32),
                        pltpu.VMEM((ns, t, V_B), F32)],
        compiler_params=_cparams(("arbitrary", "arbitrary")),
        name="gdn_scan",
    )(p, p, p, p, ps, conv_w, conv_w, conv_w, ng, s0f, s0b)


def _rotary(x, cos, sin_signed):
    even = (_iota2(x.shape, 1) % 2) == 0
    swapped = jnp.where(even, pltpu.roll(x, LANE - 1, 1), pltpu.roll(x, 1, 1))
    return x * cos + swapped * sin_signed


def _ret_kernel(q_ref, k_ref, v_ref, z_ref, cos_ref, sin_ref, ng_ref, nb_ref, s0f_ref, s0b_ref,
                y_ref, sf_ref, sb_ref, of_scr, ob_scr, *, t, c, on_grid):
    n = t // c
    h = pl.program_id(1)
    hf = jnp.full((1, 1), h, jnp.int32).astype(F32)
    lg_f = jnp.log1p(-jnp.exp2(-5.0 - hf))
    lg_b = jnp.log1p(-jnp.exp2(-5.0 - (H_C - 1.0 - hf)))
    row, col = _iota2((c, c), 0), _iota2((c, c), 1)
    dmat_f = jnp.where(col <= row, jnp.exp((row - col).astype(F32) * lg_f), 0.0)
    dmat_b = jnp.where(col >= row, jnp.exp((col - row).astype(F32) * lg_b), 0.0)
    r1 = _iota2((c, 1), 0).astype(F32)
    eg_f, ed_f = jnp.exp((r1 + 1.0) * lg_f), jnp.exp((c - 1.0 - r1) * lg_f)
    eg_b, ed_b = jnp.exp((c - r1) * lg_b), jnp.exp(r1 * lg_b)
    tot_f, tot_b = jnp.exp(c * lg_f), jnp.exp(c * lg_b)

    def load_qk(start):
        q = q_ref[pl.ds(start, c), :]
        k = k_ref[pl.ds(start, c), :] * (K_C ** -0.5)
        if on_grid:
            cs, sn = cos_ref[pl.ds(start, c), :], sin_ref[pl.ds(start, c), :]
            q, k = _rotary(q, cs, sn), _rotary(k, cs, sn)
        return q, k

    def finish(a):
        o = of_scr[pl.ds(a, c), :] + ob_scr[pl.ds(a, c), :]
        mu = jnp.mean(o, axis=-1, keepdims=True)
        d = o - mu
        var = jnp.mean(d * d, axis=-1, keepdims=True)
        y = d * lax.rsqrt(var + 1e-5) * ng_ref[...] + nb_ref[...]
        y_ref[pl.ds(a, c), :] = (y * _silu(z_ref[pl.ds(a, c), :])).astype(y_ref.dtype)

    def body(i, carry, done):
        st_f, st_b = carry
        a = _aligned(i * c, c)
        b = _aligned((n - 1 - i) * c, c)
        (qf, kf), (qb, kb) = load_qk(a), load_qk(b)
        vf, vb = v_ref[pl.ds(a, c), :], v_ref[pl.ds(b, c), :]
        sc_f, sc_b = _mm_nt(qf, kf), _mm_nt(qb, kb)
        in_f, in_b = _mm(qf * eg_f, st_f), _mm(qb * eg_b, st_b)
        up_f, up_b = _mm_tn(kf * ed_f, vf), _mm_tn(kb * ed_b, vb)
        of_scr[pl.ds(a, c), :] = _mm(sc_f * dmat_f, vf) + in_f
        ob_scr[pl.ds(b, c), :] = _mm(sc_b * dmat_b, vb) + in_b
        _finish_visited([(a, 0), (b, 1)], done, finish)
        return st_f * tot_f + up_f, st_b * tot_b + up_b

    st_f, st_b = _run_trips(n, body, (s0f_ref[0, 0], s0b_ref[0, 0]))
    sf_ref[0, 0] = st_f
    sb_ref[0, 0] = st_b


def _ret(p, b, t, cos_t, sin_t, ng, nb, s0f, s0b, on_grid):
    c = min(CHUNK_C, t)
    q_off, k_off = 0, WC_QK // K_C
    v_off, z_off = (2 * WC_QK) // V_C, (2 * WC_QK + WC_V) // V_C
    st_shape = jax.ShapeDtypeStruct((b, H_C, K_C, V_C), F32)
    st_spec = pl.BlockSpec((1, 1, K_C, V_C), lambda i, h: (i, h, 0, 0))
    tab_spec = pl.BlockSpec((t, K_C), lambda i, h: (0, 0))
    vrow = pl.BlockSpec((1, V_C), lambda i, h: (0, 0))
    return pl.pallas_call(
        functools.partial(_ret_kernel, t=t, c=c, on_grid=on_grid),
        out_shape=(jax.ShapeDtypeStruct((b * t, WC_V), MXU_DTYPE), st_shape, st_shape),
        grid=(b, H_C),
        in_specs=[pl.BlockSpec((t, K_C), lambda i, h: (i, q_off + h)),
                  pl.BlockSpec((t, K_C), lambda i, h: (i, k_off + h)),
                  pl.BlockSpec((t, V_C), lambda i, h: (i, v_off + h)),
                  pl.BlockSpec((t, V_C), lambda i, h: (i, z_off + h)),
                  tab_spec, tab_spec, vrow, vrow, st_spec, st_spec],
        out_specs=(pl.BlockSpec((t, V_C), lambda i, h: (i, h)), st_spec, st_spec),
        scratch_shapes=[pltpu.VMEM((t, V_C), F32), pltpu.VMEM((t, V_C), F32)],
        compiler_params=_cparams(("arbitrary", "arbitrary")),
        name="ret_scan",
    )(p, p, p, p, cos_t, sin_t, ng, nb, s0f, s0b)


def _ssd_chunks(chains, states, consts, expands, c):
    hp = HEADS_PER_GROUP
    w = hp * P_D
    gates = [_sel2_r(ch["dt_la"], expands[ch["rev"]]) for ch in chains]
    for ch, gt in zip(chains, gates):
        ch["v"], ch["la_x"] = ch["x"] * gt[:, :w], gt[:, w:]
    cums = [_sel2_l(consts[ch["rev"]][0], jnp.concatenate([ch["la_x"] * consts[ch["rev"]][1], ch["la_x"]], axis=1))
            for ch in chains]
    scores = [_mm_nt(ch["cm"], jnp.concatenate([ch["bd"]] * hp, axis=0)) for ch in chains]
    for ch, x, sc in zip(chains, cums, scores):
        gx = x[:, w:]
        ch["gx"], ch["gl"] = gx, (gx[0:1] if ch["rev"] else gx[c - 1:c])
        ch["p"] = sc * jnp.exp(x[:, :w]) * consts[ch["rev"]][2]
    intra = [_mm(ch["p"], jnp.concatenate([ch["v"]] * hp, axis=0) * consts[ch["rev"]][3]) for ch in chains]
    upd = [_mm_tn(ch["bd"], ch["v"] * jnp.exp(ch["gl"] - ch["gx"])) for ch in chains]
    outs, states = [], dict(states)
    for ch, oi, up in zip(chains, intra, upd):
        st = states[ch["rev"]]
        outs.append(oi + _mm(ch["cm"], st) * jnp.exp(ch["gx"]))
        states[ch["rev"]] = st * jnp.exp(ch["gl"]) + up
    return outs, states


def _ssd_kernel(x_ref, b_ref, c_ref, z_ref, sm_ref, wx_ref, wb_ref, wc_ref, bx_ref, bb_ref, bc_ref,
                dsk_ref, ng_ref, s0f_ref, s0b_ref,
                y_ref, sf_ref, sb_ref, xs, bs, cs, of_scr, ob_scr, *, t, c):
    n = t // c
    hp = HEADS_PER_GROUP
    w = hp * P_D
    grp = pl.program_id(1)
    erow, ecol = _iota2((LANE, 2 * w), 0), _iota2((LANE, 2 * w), 1)
    head = grp * hp + (ecol % w) // P_D + jnp.where(ecol < w, 0, CD_LA)
    expand_f = _f01(erow == CD_DTF + head)
    expand_b = _f01(erow == CD_DTB + head)

    def conv_inputs(a):
        x = _silu(_conv3_chunk(x_ref, wx_ref, a, c, t, bx_ref[...]))
        bd = _silu(_conv3_chunk(b_ref, wb_ref, a, c, t, bb_ref[...]))
        cm = _silu(_conv3_chunk(c_ref, wc_ref, a, c, t, bc_ref[...]))
        xs[pl.ds(a, c), :], bs[pl.ds(a, c), :], cs[pl.ds(a, c), :] = x, bd, cm
        return x, bd, cm

    row, col = _iota2((c, c), 0), _iota2((c, c), 1)
    rt, ct = _iota2((c, hp * c), 0), _iota2((c, hp * c), 1) % c
    brow, bcol = _iota2((hp * c, w), 0), _iota2((hp * c, w), 1)
    bd_mask = _f01((brow // c) == (bcol // P_D))
    consts = {False: (_f01(col <= row), _f01(rt > ct), _f01(ct <= rt), bd_mask),
              True: (_f01(col >= row), _f01(rt < ct), _f01(ct >= rt), bd_mask)}
    expands = {False: expand_f, True: expand_b}
    o_scrs = {False: of_scr, True: ob_scr}

    def finish(a):
        o = of_scr[pl.ds(a, c), :] + ob_scr[pl.ds(a, c), :] + dsk_ref[...] * xs[pl.ds(a, c), :]
        o = o * _silu(z_ref[pl.ds(a, c), :])
        y = o * lax.rsqrt(jnp.mean(o * o, axis=-1, keepdims=True) + 1e-6) * ng_ref[...]
        y_ref[pl.ds(a, c), :] = y.astype(y_ref.dtype)

    def body(i, carry, done):
        chains = []
        if done == "same":
            fresh = {ch: conv_inputs(ch * c) for ch in range(SSD_CHUNKS)}
        for cc in range(SSD_CHUNKS):
            for rev in (False, True):
                ci = i * SSD_CHUNKS + cc
                chunk = n - 1 - ci if rev else ci
                a = _aligned(chunk * c, c)
                if done is None:
                    x, bd, cm = conv_inputs(a)
                elif done == "same":
                    x, bd, cm = fresh[chunk]
                else:
                    x, bd, cm = xs[pl.ds(a, c), :], bs[pl.ds(a, c), :], cs[pl.ds(a, c), :]
                chains.append(dict(a=a, rev=rev, cm=cm, bd=bd, x=x, dt_la=sm_ref[pl.ds(a, c), :]))
        outs, states = _ssd_chunks(chains, {False: carry[0], True: carry[1]}, consts, expands, c)
        for ch, o in zip(chains, outs):
            o_scrs[ch["rev"]][pl.ds(ch["a"], c), :] = o
        _finish_visited([(ch["a"], int(ch["rev"])) for ch in chains], done, finish)
        return states[False], states[True]

    st_f, st_b = _run_trips(n // SSD_CHUNKS, body, (s0f_ref[0, 0], s0b_ref[0, 0]))
    sf_ref[0, 0] = st_f
    sb_ref[0, 0] = st_b


def _ssd(p, ps, b, t, conv_w, conv_b, dskip, ng, s0f, s0b):
    c = CHUNK_D
    w = HEADS_PER_GROUP * P_D
    base = 2 * WC_QK + 2 * WC_V
    x_off = base // w
    b_off = (base + WD) // N_D
    c_off = (base + WD + WD_BC) // N_D
    z_off = (base + WD + 2 * WD_BC) // w
    sm_blk = CD_MAIN // LANE
    st_shape = jax.ShapeDtypeStruct((b, G_D, N_D, w), F32)
    st_spec = pl.BlockSpec((1, 1, N_D, w), lambda i, g: (i, g, 0, 0))
    row_spec = pl.BlockSpec((1, LANE), lambda i, g: (0, 0))
    return pl.pallas_call(
        functools.partial(_ssd_kernel, t=t, c=c),
        out_shape=(jax.ShapeDtypeStruct((b * t, WD), MXU_DTYPE), st_shape, st_shape),
        grid=(b, G_D),
        in_specs=[pl.BlockSpec((t, w), lambda i, g: (i, x_off + g)),
                  pl.BlockSpec((t, N_D), lambda i, g: (i, b_off + g)),
                  pl.BlockSpec((t, N_D), lambda i, g: (i, c_off + g)),
                  pl.BlockSpec((t, w), lambda i, g: (i, z_off + g)),
                  pl.BlockSpec((t, LANE), lambda i, g: (i, 0)),
                  pl.BlockSpec((3, w), lambda i, g: (0, g)),
                  pl.BlockSpec((3, N_D), lambda i, g: (0, WD // N_D + g)),
                  pl.BlockSpec((3, N_D), lambda i, g: (0, (WD + WD_BC) // N_D + g)),
                  pl.BlockSpec((1, w), lambda i, g: (0, g)),
                  pl.BlockSpec((1, N_D), lambda i, g: (0, WD // N_D + g)),
                  pl.BlockSpec((1, N_D), lambda i, g: (0, (WD + WD_BC) // N_D + g)),
                  pl.BlockSpec((1, w), lambda i, g: (0, g)),
                  pl.BlockSpec((1, w), lambda i, g: (0, g)),
                  st_spec, st_spec],
        out_specs=(pl.BlockSpec((t, w), lambda i, g: (i, g)), st_spec, st_spec),
        scratch_shapes=[pltpu.VMEM((t, w), F32), pltpu.VMEM((t, N_D), F32), pltpu.VMEM((t, N_D), F32),
                        pltpu.VMEM((t, w), F32), pltpu.VMEM((t, w), F32)],
        compiler_params=_cparams(("arbitrary", "arbitrary")),
        name="ssd_scan",
    )(p, p, p, p, ps, conv_w, conv_w, conv_w, conv_b, conv_b, conv_b, dskip, ng, s0f, s0b)


def _split_cols(w, sizes):
    out, o = [], 0
    for s in sizes:
        out.append(w[..., o:o + s])
        o += s
    return out


def _pad_cols(w, n):
    return jnp.pad(w, [(0, 0)] * (w.ndim - 1) + [(0, n - w.shape[-1])])


def _lane_row(pieces):
    row = jnp.zeros((LANE,), F32)
    for off, vec in pieces:
        row = lax.dynamic_update_slice(row, vec.astype(F32), (off,))
    return row.reshape(1, LANE)


def _rotary_tables(t):
    n_rows = t // GRID_W
    rows = jnp.repeat(jnp.arange(n_rows), GRID_W).astype(F32)
    cols = jnp.tile(jnp.arange(GRID_W), n_rows).astype(F32)
    n_freq = K_C // 4
    inv_freq = ROPE_BASE ** (-jnp.arange(n_freq, dtype=F32) / n_freq)
    ang = jnp.concatenate([rows[:, None] * inv_freq, cols[:, None] * inv_freq], axis=-1)
    cos = jnp.repeat(jnp.cos(ang), 2, axis=-1)
    sin = jnp.repeat(jnp.sin(ang), 2, axis=-1)
    sign = jnp.tile(jnp.array([-1.0, 1.0], F32), K_C // 2)
    return cos, sin * sign


def kernel(x_prompt, x_sample, state_gla_fwd, state_gla_bwd, state_gdn_fwd, state_gdn_bwd, state_ret_fwd, state_ret_bwd, state_ssd_fwd, state_ssd_bwd, c, c_ctx, ada_w, ada_b, norm_g, final_norm_g, ab_w_in, ab_w_out, gla_gate_w2, gla_gate_b, gla_norm_g, gdn_conv_w, gdn_a_log, gdn_dt_bias, gdn_norm_g, cd_w_in, cd_w_out, ret_norm_g, ret_norm_b, ssd_conv_w, ssd_conv_b, ssd_a_log, ssd_dt_bias, ssd_d, ssd_norm_g):
    bp, tp, _ = x_prompt.shape
    bs, ts, _ = x_sample.shape
    hp = HEADS_PER_GROUP

    cvec = jnp.concatenate([c_ctx[None, :], c], axis=0)
    cvec16 = jnp.pad(cvec, ((0, 16 - cvec.shape[0]), (0, 0)))
    mod = _ada_mod(cvec16, ada_w, ada_b).reshape(DEPTH, 16, 3, D_MODEL)

    groups = [
        dict(x=x_prompt, b=bp, t=tp, shared=True, on_grid=False, rows=slice(0, 1)),
        dict(x=x_sample, b=bs, t=ts, shared=False, on_grid=True, rows=slice(1, 1 + bs)),
    ]
    cos_t, sin_t = _rotary_tables(ts)
    xs = [g["x"].reshape(g["b"] * g["t"], D_MODEL) for g in groups]
    new_states = {}

    for l in range(DEPTH):
        i = l // 2
        if l % 2 == 0:
            w = ab_w_in[i]
            w_in, w_small = _regroup_ab_weight(w)
            w_out = ab_w_out[i].astype(MXU_DTYPE)
            wg = jnp.zeros((2, LANE, WA_QK), F32)
            wg = wg.at[0, AB_RF:AB_RF + GLA_RANK].set(gla_gate_w2[i, 0])
            wg = wg.at[1, AB_RB:AB_RB + GLA_RANK].set(gla_gate_w2[i, 1])
            bg = gla_gate_b[i].reshape(2, 1, WA_QK)
            dtb_row = _lane_row([(AB_AF, gdn_dt_bias[i, 0]), (AB_AB, gdn_dt_bias[i, 1])])
            alog_row = _lane_row([(AB_AF, gdn_a_log[i, 0]), (AB_AB, gdn_a_log[i, 1])])
        else:
            w = cd_w_in[i]
            w_in = w.astype(MXU_DTYPE)
            w_small = jnp.concatenate([w[:, CD_MAIN:].astype(MXU_DTYPE),
                                       jnp.zeros((D_MODEL, LANE - 2 * H_D), MXU_DTYPE)], axis=1)
            w_out = cd_w_out[i].astype(MXU_DTYPE)
            dtb_row = _lane_row([(CD_DTF, ssd_dt_bias[i, 0]), (CD_DTB, ssd_dt_bias[i, 1])])
            alog_row = _lane_row([(CD_DTF, ssd_a_log[i, 0]), (CD_DTB, ssd_a_log[i, 1])])
            dskip = jnp.repeat(ssd_d[i], P_D).reshape(1, WD)

        for gi, g in enumerate(groups):
            b, t = g["b"], g["t"]
            mod_g = mod[l, g["rows"]]
            h = _norm_mod(xs[gi].reshape(b, t, D_MODEL), norm_g[l], mod_g, g["shared"])
            p, ps = _proj(h, w_in, w_small, dtb_row, alog_row, "ab" if l % 2 == 0 else "cd")
            if l % 2 == 0:
                if gi == 0:
                    sa_f = sa_b = jnp.zeros((b, H_A, V_A, K_A), F32)
                    sb_f = sb_b = jnp.zeros((b, H_B, K_B, V_B), F32)
                else:
                    sa_f = jnp.swapaxes(state_gla_fwd[:, i], -1, -2)
                    sa_b = jnp.swapaxes(state_gla_bwd[:, i], -1, -2)
                    sb_f, sb_b = state_gdn_fwd[:, i], state_gdn_bwd[:, i]
                y1, fa_f, fa_b = _gla(p, ps, b, t, wg, bg, gla_norm_g[i].reshape(1, V_A), sa_f, sa_b)
                y2, fb_f, fb_b = _gdn(p, ps, b, t, gdn_conv_w[i], gdn_norm_g[i].reshape(1, V_B), sb_f, sb_b)
                if gi == 0:
                    new_states["gla_f"] = jnp.swapaxes(fa_f, -1, -2)
                    new_states["gla_b"] = jnp.swapaxes(fa_b, -1, -2)
                    new_states["gdn_f"], new_states["gdn_b"] = fb_f, fb_b
            else:
                if gi == 0:
                    sc_f = sc_b = jnp.zeros((b, H_C, K_C, V_C), F32)
                    sd_f = sd_b = jnp.zeros((b, G_D, N_D, hp * P_D), F32)
                else:
                    sc_f, sc_b = state_ret_fwd[:, i], state_ret_bwd[:, i]

                    def to_grp(s):
                        s = s.reshape(b, G_D, hp, N_D, P_D)
                        return jnp.transpose(s, (0, 1, 3, 2, 4)).reshape(b, G_D, N_D, hp * P_D)

                    sd_f, sd_b = to_grp(state_ssd_fwd[:, i]), to_grp(state_ssd_bwd[:, i])
                tab_c, tab_s = (cos_t, sin_t) if g["on_grid"] else (cos_t[:t], sin_t[:t])
                y1, fc_f, fc_b = _ret(p, b, t, tab_c, tab_s, ret_norm_g[i].reshape(1, V_C),
                                      ret_norm_b[i].reshape(1, V_C), sc_f, sc_b, g["on_grid"])
                y2, fd_f, fd_b = _ssd(p, ps, b, t, ssd_conv_w[i], ssd_conv_b[i].reshape(1, -1),
                                      dskip, ssd_norm_g[i].reshape(1, WD), sd_f, sd_b)
                if gi == 0:
                    def from_grp(s):
                        s = s.reshape(b, G_D, N_D, hp, P_D)
                        return jnp.transpose(s, (0, 1, 3, 2, 4)).reshape(b, H_D, N_D, P_D)

                    new_states["ret_f"], new_states["ret_b"] = fc_f, fc_b
                    new_states["ssd_f"], new_states["ssd_b"] = from_grp(fd_f), from_grp(fd_b)
            xs[gi] = _out_proj(y1, y2, w_out, xs[gi], mod_g, t, g["shared"])

    y_prompt = _final_norm(xs[0], final_norm_g).reshape(bp, tp, D_MODEL)
    y_sample = _final_norm(xs[1], final_norm_g).reshape(bs, ts, D_MODEL)
    st = lambda k: new_states[k][:, None]
    return (y_prompt, y_sample, st("gla_f"), st("gla_b"), st("gdn_f"), st("gdn_b"),
            st("ret_f"), st("ret_b"), st("ssd_f"), st("ssd_b"))
```
